```python
import jax
import jax.numpy as jnp
from jax import lax
import numpy as np


D_MODEL = 1024
BATCH = 8
SEQ = 4096
DEPTH = 2

GRID_W = 64
CTX_LEN = 256
MIX_W = D_MODEL
FOURIER_W = MIX_W // 2
FOURIER_GROUPS = 4
DN_W = MIX_W - FOURIER_W
DN_HEADS = 4
DN_HEAD_DIM = DN_W // DN_HEADS
CONV_K = 5
CHUNK = 64
N_EXPERTS = 32
TOP_K = 4
D_FF_EXPERT = D_MODEL
SWIGLU_LIMIT = 7.0
SWIGLU_ALPHA = 1.702
MOE_BLOCK = 256
IN_W = FOURIER_W + 4 * DN_W + 4 * DN_HEADS
EPS = 1e-6

kernel_name = "hybrid_fourier_gdn_moe_prefix_dit"


def _rms_norm(x, g):
    xf = x.astype(jnp.float32)
    y = xf * lax.rsqrt(jnp.mean(xf * xf, axis=-1, keepdims=True) + EPS) * g.astype(jnp.float32)
    return y.astype(x.dtype)


def _split_mod(m):
    return jnp.split(m[..., None, :], 6, axis=-1)


def _short_conv(u, w):
    ch = u.shape[-1]
    return lax.conv_general_dilated(
        u, w[:, None, :].astype(u.dtype), window_strides=(1,),
        padding=[(CONV_K // 2, CONV_K // 2)],
        dimension_numbers=('NWC', 'WIO', 'NWC'), feature_group_count=ch)


def _l2norm(t):
    return t * lax.rsqrt(jnp.sum(t * t, axis=-1, keepdims=True) + EPS)


def _fourier(f):
    b, l, _ = f.shape
    fg = f.astype(jnp.float32).reshape(b, l, FOURIER_GROUPS, FOURIER_W // FOURIER_GROUPS)
    return jnp.fft.fftn(fg, axes=(1, 3), norm='ortho').real.reshape(b, l, FOURIER_W)


def _dn_inputs(u, conv_w):
    b, l, _ = u.shape
    u = u.astype(jnp.float32)
    qkv = jax.nn.silu(_short_conv(u[..., :3 * DN_W], conv_w.astype(jnp.float32)))
    q, k, v = jnp.split(qkv, 3, axis=-1)
    heads = lambda t: t.reshape(b, l, DN_HEADS, DN_HEAD_DIM).transpose(0, 2, 1, 3)
    q = _l2norm(heads(q)) * (DN_HEAD_DIM ** -0.5)
    k = _l2norm(heads(k))
    v = heads(v)
    z = u[..., 3 * DN_W:4 * DN_W]
    ab = u[..., 4 * DN_W:].reshape(b, l, 2, 2, DN_HEADS)
    return q, k, v, z, ab


def _gated_delta_chunked(q, k, v, g, beta, s0):
    b, h, l, dk = q.shape
    dv = v.shape[-1]
    n = l // CHUNK
    q = q.reshape(b, h, n, CHUNK, dk)
    k = k.reshape(b, h, n, CHUNK, dk)
    v = v.reshape(b, h, n, CHUNK, dv)
    beta = beta.reshape(b, h, n, CHUNK)
    gam = jnp.cumsum(g.reshape(b, h, n, CHUNK), axis=-1)
    idx = jnp.arange(CHUNK)
    incl = idx[:, None] >= idx[None, :]
    strict = idx[:, None] > idx[None, :]
    decay = jnp.exp(jnp.where(incl, gam[..., :, None] - gam[..., None, :], -jnp.inf))
    kb = k * beta[..., None]
    lmat = jnp.where(strict, jnp.einsum('bhnid,bhnjd->bhnij', kb, k) * decay, 0.0)
    eye = jnp.eye(CHUNK, dtype=q.dtype)
    tmat = lax.linalg.triangular_solve(lmat + eye, jnp.broadcast_to(eye, lmat.shape),
                                       left_side=True, lower=True, unit_diagonal=True)
    u_val = tmat @ (v * beta[..., None])
    w_k = tmat @ (kb * jnp.exp(gam)[..., None])
    attn = jnp.einsum('bhnid,bhnjd->bhnij', q, k) * decay
    qg = q * jnp.exp(gam)[..., None]
    kend = k * jnp.exp(gam[..., -1:] - gam)[..., None]
    gend = jnp.exp(gam[..., -1])

    def step(s, xs):
        u_i, w_i, a_i, qg_i, ke_i, ge_i = xs
        v_new = u_i - w_i @ s
        o_i = qg_i @ s + a_i @ v_new
        s = s * ge_i[..., None, None] + jnp.einsum('bhck,bhcv->bhkv', ke_i, v_new)
        return s, o_i

    xs = tuple(jnp.moveaxis(t, 2, 0) for t in (u_val, w_k, attn, qg, kend, gend))
    s_fin, o = lax.scan(step, s0, xs)
    o = jnp.moveaxis(o, 0, 2).reshape(b, h, l, dv)
    return o, s_fin


def _bidir_delta(q, k, v, ab, a_log, dt_bias, s0_fwd, s0_bwd):
    a_log = a_log.astype(jnp.float32)
    dt_bias = dt_bias.astype(jnp.float32)
    outs, states = [], []
    for d, s0 in ((0, s0_fwd), (1, s0_bwd)):
        g = -jnp.exp(a_log[d])[None, :, None] * jax.nn.softplus(
            ab[:, :, d, 0, :].transpose(0, 2, 1) + dt_bias[d][None, :, None])
        beta = jax.nn.sigmoid(ab[:, :, d, 1, :].transpose(0, 2, 1))
        args = (q, k, v, g, beta)
        if d == 1:
            args = tuple(jnp.flip(t, axis=2) for t in args)
        o, s = _gated_delta_chunked(*args, s0)
        if d == 1:
            o = jnp.flip(o, axis=2)
        outs.append(o)
        states.append(s)
    return outs[0] + outs[1], states[0], states[1]


def _gated_out(o, z, g_out):
    b, h, l, dv = o.shape
    o = o.transpose(0, 2, 1, 3)
    o = o * lax.rsqrt(jnp.mean(o * o, axis=-1, keepdims=True) + EPS) * g_out.astype(jnp.float32)
    o = o * jax.nn.silu(z.reshape(b, l, h, dv))
    return o.reshape(b, l, h * dv)


def _moe(h, w_router, b_router, w_gate, b_gate, w_up, b_up, w_down, b_down):
    t, d = h.shape
    logits = (h @ w_router + b_router).astype(jnp.float32)
    top_v, top_i = lax.top_k(logits, TOP_K)
    gates = jax.nn.softmax(top_v, axis=-1)
    tk = t * TOP_K
    flat_e = top_i.reshape(tk)
    flat_tok = jnp.repeat(jnp.arange(t, dtype=jnp.int32), TOP_K)
    flat_g = gates.reshape(tk)
    order = jnp.argsort(flat_e)
    se = flat_e[order]
    counts = jnp.bincount(flat_e, length=N_EXPERTS)
    group_start = jnp.cumsum(counts) - counts
    padded = (counts + MOE_BLOCK - 1) // MOE_BLOCK * MOE_BLOCK
    padded_end = jnp.cumsum(padded)
    padded_start = padded_end - padded
    dest = padded_start[se] + jnp.arange(tk) - group_start[se]
    n_blocks = -(-tk // MOE_BLOCK) + N_EXPERTS
    p = n_blocks * MOE_BLOCK
    slot_tok = jnp.full((p,), t, jnp.int32).at[dest].set(flat_tok[order])
    slot_gate = jnp.zeros((p,), jnp.float32).at[dest].set(flat_g[order])
    block_e = jnp.minimum(
        jnp.searchsorted(padded_end, jnp.arange(n_blocks) * MOE_BLOCK, side='right'), N_EXPERTS - 1)
    h_pad = jnp.concatenate([h, jnp.zeros((1, d), h.dtype)], axis=0)

    def expert_block(args):
        tok, gate, e = args
        xb = h_pad[tok]
        a = jnp.minimum(xb @ w_gate[e] + b_gate[e], SWIGLU_LIMIT)
        u = jnp.clip(xb @ w_up[e] + b_up[e], -SWIGLU_LIMIT, SWIGLU_LIMIT)
        y = (a * jax.nn.sigmoid(SWIGLU_ALPHA * a) * (u + 1.0)) @ w_down[e] + b_down[e]
        return y * gate[:, None].astype(y.dtype)

    y = lax.map(expert_block, (slot_tok.reshape(n_blocks, MOE_BLOCK),
                               slot_gate.reshape(n_blocks, MOE_BLOCK), block_e))
    return jnp.zeros((t + 1, d), h.dtype).at[slot_tok].add(y.reshape(p, d))[:t]


def _layer(x, xc, c, c_ctx, w_mod, b_mod, g_norm1, w_in, conv_w, a_log, dt_bias, g_out_norm,
           w_out, g_norm2, w_router, b_router, w_gate, b_gate, w_up, b_up, w_down, b_down,
           update_ctx):
    dt = x.dtype
    bsz, seq, d = x.shape
    sh1, sc1, gt1, sh2, sc2, gt2 = _split_mod(jax.nn.silu(c) @ w_mod + b_mod)
    sh1c, sc1c, gt1c, sh2c, sc2c, gt2c = _split_mod(jax.nn.silu(c_ctx) @ w_mod + b_mod)

    hx = _rms_norm(x, g_norm1) * (1.0 + sc1) + sh1
    hc = _rms_norm(xc, g_norm1) * (1.0 + sc1c) + sh1c
    ux = hx @ w_in
    uc = hc @ w_in

    qc, kc, vc, zc, abc = _dn_inputs(uc[..., FOURIER_W:], conv_w)
    s_zero = jnp.zeros((xc.shape[0], DN_HEADS, DN_HEAD_DIM, DN_HEAD_DIM), jnp.float32)
    oc, s_ctx_f, s_ctx_b = _bidir_delta(qc, kc, vc, abc, a_log, dt_bias, s_zero, s_zero)

    qx, kx, vx, zx, abx = _dn_inputs(ux[..., FOURIER_W:], conv_w)
    ox, _, _ = _bidir_delta(qx, kx, vx, abx, a_log, dt_bias, s_ctx_f, s_ctx_b)

    mix_x = jnp.concatenate([_fourier(ux[..., :FOURIER_W]), _gated_out(ox, zx, g_out_norm)],
                            axis=-1).astype(dt) @ w_out
    x = x + gt1 * mix_x
    h2x = _rms_norm(x, g_norm2) * (1.0 + sc2) + sh2

    if update_ctx:
        mix_c = jnp.concatenate([_fourier(uc[..., :FOURIER_W]), _gated_out(oc, zc, g_out_norm)],
                                axis=-1).astype(dt) @ w_out
        xc = xc + gt1c * mix_c
        h2c = _rms_norm(xc, g_norm2) * (1.0 + sc2c) + sh2c
        n_lat = bsz * seq
        tokens = jnp.concatenate([h2x.reshape(n_lat, d), h2c.reshape(-1, d)], axis=0)
        y = _moe(tokens, w_router, b_router, w_gate, b_gate, w_up, b_up, w_down, b_down)
        x = x + gt2 * y[:n_lat].reshape(bsz, seq, d)
        xc = xc + gt2c * y[n_lat:].reshape(xc.shape)
    else:
        y = _moe(h2x.reshape(bsz * seq, d), w_router, b_router, w_gate, b_gate, w_up, b_up,
                 w_down, b_down)
        x = x + gt2 * y.reshape(bsz, seq, d)
    return x, xc


def setup_inputs(seed: int = 0) -> dict:
    key = jax.random.key(seed)
    ks = jax.random.split(key, 26)
    d, nl, e, f = D_MODEL, DEPTH, N_EXPERTS, D_FF_EXPERT
    f32 = jnp.float32

    def nrm(k, shape, fan_in, s=1.0):
        return s * jax.random.normal(k, shape, f32) * (fan_in ** -0.5)

    def small(k, shape, s=0.02):
        return s * jax.random.normal(k, shape, f32)

    dt_init = jnp.exp(jax.random.uniform(ks[10], (nl, 2, DN_HEADS), f32,
                                         jnp.log(1e-3), jnp.log(1e-1)))
    return {
        'x': jax.random.normal(ks[0], (BATCH, SEQ, d), f32),
        'c': jax.random.normal(ks[1], (BATCH, d), f32),
        'ctx': jax.random.normal(ks[2], (BATCH, CTX_LEN, d), f32),
        'c_ctx': jax.random.normal(ks[3], (d,), f32),
        'w_mod': nrm(ks[4], (nl, d, 6 * d), d, 0.5),
        'b_mod': small(ks[5], (nl, 6 * d)),
        'g_norm1': 1.0 + small(ks[6], (nl, d)),
        'w_in': nrm(ks[7], (nl, d, IN_W), d),
        'conv_w': nrm(ks[8], (nl, CONV_K, 3 * DN_W), CONV_K),
        'a_log': jnp.log(jax.random.uniform(ks[9], (nl, 2, DN_HEADS), f32, 1.0, 16.0)),
        'dt_bias': dt_init + jnp.log(-jnp.expm1(-dt_init)),
        'g_out_norm': 1.0 + small(ks[11], (nl, DN_HEAD_DIM)),
        'w_out': nrm(ks[12], (nl, MIX_W, d), MIX_W),
        'g_norm2': 1.0 + small(ks[13], (nl, d)),
        'w_router': nrm(ks[14], (nl, d, e), d),
        'b_router': small(ks[15], (nl, e), 0.01),
        'w_gate': nrm(ks[16], (nl, e, d, f), d),
        'b_gate': small(ks[17], (nl, e, f)),
        'w_up': nrm(ks[18], (nl, e, d, f), d),
        'b_up': small(ks[19], (nl, e, f)),
        'w_down': nrm(ks[20], (nl, e, f, d), f),
        'b_down': small(ks[21], (nl, e, d)),
        'g_final': 1.0 + small(ks[22], (d,)),
    }


def reference(x, c, ctx, c_ctx, w_mod, b_mod, g_norm1, w_in, conv_w, a_log, dt_bias,
              g_out_norm, w_out, g_norm2, w_router, b_router, w_gate, b_gate, w_up, b_up,
              w_down, b_down, g_final):
    xc = ctx
    for l in range(DEPTH):
        x, xc = _layer(x, xc, c, c_ctx, w_mod[l], b_mod[l], g_norm1[l], w_in[l], conv_w[l],
                       a_log[l], dt_bias[l], g_out_norm[l], w_out[l], g_norm2[l], w_router[l],
                       b_router[l], w_gate[l], b_gate[l], w_up[l], b_up[l], w_down[l], b_down[l],
                       l < DEPTH - 1)
    return _rms_norm(x, g_final)
```

```python
import functools
import math

import numpy as np
import jax
import jax.numpy as jnp
from jax import lax
from jax.experimental import pallas as pl
from jax.experimental.pallas import tpu as pltpu

FOURIER_GROUPS = 4
DN_HEADS = 4
DN_HEAD_DIM = 128
CONV_K = 5
N_EXPERTS = 32
TOP_K = 4
SWIGLU_LIMIT = 7.0
SWIGLU_ALPHA = 1.702
EPS = 1e-6

LANES = 128
SUBLANES = 8
V7X_VMEM_BYTES = 64 * 1024 * 1024
VMEM_LIMIT = V7X_VMEM_BYTES - 8 * 1024 * 1024
CHUNK = 64
GROUP = 4 * CHUNK
MOE_BLOCK = 256

F32 = jnp.float32
BF16 = jnp.bfloat16


def _cparams(*sem):
    return pltpu.CompilerParams(dimension_semantics=sem, vmem_limit_bytes=VMEM_LIMIT)


def _sigmoid(x):
    return 1.0 / (1.0 + jnp.exp(-x))


def _mod_kernel(c_ref, w_ref, b_ref, o_ref):
    c = c_ref[...]
    a = c * _sigmoid(c)
    o_ref[...] = jnp.dot(a, w_ref[...], preferred_element_type=F32) + b_ref[...]


def _mod(c_rows, w_mod, b_mod):
    r, d = c_rows.shape
    n = w_mod.shape[1]
    tn = n // 4
    return pl.pallas_call(
        _mod_kernel,
        grid=(n // tn,),
        in_specs=[pl.BlockSpec((r, d), lambda j: (0, 0)),
                  pl.BlockSpec((d, tn), lambda j: (0, j)),
                  pl.BlockSpec((1, tn), lambda j: (0, j))],
        out_specs=pl.BlockSpec((r, tn), lambda j: (0, j)),
        out_shape=jax.ShapeDtypeStruct((r, n), F32),
        compiler_params=_cparams("parallel"),
        name="mod",
    )(c_rows, w_mod, b_mod.reshape(1, n))


def _inproj_kernel(x_ref, g_ref, sc_ref, sh_ref, w_ref, wab_ref, alog_ref, dtb_ref, isdec_ref,
                   f_ref, qkv_ref, z_ref, gb_ref, *, fw, qw):
    x = x_ref[...]
    inv = lax.rsqrt(jnp.mean(x * x, axis=-1, keepdims=True) + EPS)
    h = (x * inv * g_ref[...]) * (1.0 + sc_ref[...]) + sh_ref[...]
    hb = h.astype(BF16)
    f_ref[...] = jnp.dot(hb, w_ref[:, :fw], preferred_element_type=F32)
    for j in range(3):
        lo = fw + j * qw
        qkv_ref[:, j * qw:(j + 1) * qw] = jnp.dot(hb, w_ref[:, lo:lo + qw], preferred_element_type=F32)
    lo = fw + 3 * qw
    z_ref[...] = jnp.dot(hb, w_ref[:, lo:lo + qw], preferred_element_type=F32)
    ab = jnp.dot(hb, wab_ref[...], preferred_element_type=F32)
    pre = ab + dtb_ref[...]
    softplus = jnp.maximum(pre, 0.0) + jnp.log(1.0 + jnp.exp(-jnp.abs(pre)))
    dec = -jnp.exp(alog_ref[...]) * softplus
    gb = jnp.where(isdec_ref[...] > 0.0, dec, _sigmoid(ab))
    gb_ref[...] = gb[:, :gb_ref.shape[1]]


def _inproj(x, g1, sc, sh, w_main, w_ab, alog_row, dtb_row, isdec_row, rows_per_group, tm=512):
    t, d = x.shape
    ng = sc.shape[0]
    fw = FOURIER_GROUPS * LANES
    qw = DN_HEADS * DN_HEAD_DIM
    nab = 4 * DN_HEADS
    tpg = rows_per_group // tm
    grp = lambda i: (jnp.minimum(i // tpg, ng - 1), 0, 0)
    const = lambda i: (0, 0)
    return pl.pallas_call(
        functools.partial(_inproj_kernel, fw=fw, qw=qw),
        grid=(t // tm,),
        in_specs=[pl.BlockSpec((tm, d), lambda i: (i, 0)),
                  pl.BlockSpec((1, d), const),
                  pl.BlockSpec((None, 1, d), grp),
                  pl.BlockSpec((None, 1, d), grp),
                  pl.BlockSpec(w_main.shape, const),
                  pl.BlockSpec(w_ab.shape, const),
                  pl.BlockSpec((1, LANES), const),
                  pl.BlockSpec((1, LANES), const),
                  pl.BlockSpec((1, LANES), const)],
        out_specs=[pl.BlockSpec((tm, fw), lambda i: (i, 0)),
                   pl.BlockSpec((tm, 3 * qw), lambda i: (i, 0)),
                   pl.BlockSpec((tm, qw), lambda i: (i, 0)),
                   pl.BlockSpec((tm, nab), lambda i: (i, 0))],
        out_shape=[jax.ShapeDtypeStruct((t, fw), F32),
                   jax.ShapeDtypeStruct((t, 3 * qw), F32),
                   jax.ShapeDtypeStruct((t, qw), F32),
                   jax.ShapeDtypeStruct((t, nab), F32)],
        compiler_params=_cparams("parallel"),
        name="inproj",
    )(x, g1.reshape(1, d), sc, sh, w_main, w_ab, alog_row, dtb_row, isdec_row)


def _conv_act(src_ref, w_ref, dst_ref, seq_len, l2norm, scale):
    r = GROUP
    nt = seq_len // r
    halo = SUBLANES
    w = w_ref[...]

    def body(i, carry):
        r0 = pl.multiple_of(i * r, r)
        cur = src_ref[pl.ds(r0, r), :]
        p0 = pl.multiple_of(jnp.maximum(r0 - halo, 0), halo)
        n0 = pl.multiple_of(jnp.minimum(r0 + r, seq_len - halo), halo)
        prev = jnp.where(i > 0, src_ref[pl.ds(p0, halo), :], 0.0)
        nxt = jnp.where(i < nt - 1, src_ref[pl.ds(n0, halo), :], 0.0)
        win = jnp.concatenate([prev, cur, nxt], axis=0)
        n = r + 2 * halo
        acc = None
        for j in range(CONV_K):
            off = halo - CONV_K // 2 + j
            sh = pltpu.roll(win, n - off, 0)[:r]
            term = w[j:j + 1, :] * sh
            acc = term if acc is None else acc + term
        y = acc * _sigmoid(acc)
        if l2norm:
            y = y * lax.rsqrt(jnp.sum(y * y, axis=-1, keepdims=True) + EPS)
        if scale != 1.0:
            y = y * scale
        dst_ref[pl.ds(r0, r), :] = y
        return carry

    lax.fori_loop(0, nt, body, 0)


def _dot_nt(a, b):
    return lax.dot_general(a, b, (((1,), (1,)), ((), ())), preferred_element_type=F32)


def _dot_tn(a, b):
    return lax.dot_general(a, b, (((0,), (0,)), ((), ())), preferred_element_type=F32)


def _dn_group(dirn, g, state, q_s, k_s, v_s, gb_ref, gr_ref, o_ref):
    r0 = pl.multiple_of(g * GROUP, GROUP)
    q = q_s[pl.ds(r0, GROUP), :]
    k = k_s[pl.ds(r0, GROUP), :]
    v = v_s[pl.ds(r0, GROUP), :]
    gcol = gb_ref[pl.ds(r0, GROUP), 2 * dirn:2 * dirn + 1]
    bcol = gb_ref[pl.ds(r0, GROUP), 2 * dirn + 1:2 * dirn + 2]
    grow = gr_ref[dirn, pl.ds(g, 1), :]

    ii = lax.broadcasted_iota(jnp.int32, (GROUP, GROUP), 0)
    jj = lax.broadcasted_iota(jnp.int32, (GROUP, GROUP), 1)
    same = (ii // CHUNK) == (jj // CHUNK)
    if dirn == 0:
        incl, strict, incl_t = same & (ii >= jj), same & (ii > jj), same & (ii <= jj)
    else:
        incl, strict, incl_t = same & (ii <= jj), same & (ii < jj), same & (ii >= jj)

    gam_col = jnp.sum(jnp.where(incl, grow, 0.0), axis=1, keepdims=True)
    gam_row = jnp.sum(jnp.where(incl_t, gcol, 0.0), axis=0, keepdims=True)
    tot_col = jnp.sum(jnp.where(same, grow, 0.0), axis=1, keepdims=True)
    dec = jnp.where(incl, jnp.exp(jnp.where(incl, gam_col - gam_row, 0.0)), 0.0)

    kk = _dot_nt(k, k)
    qk = _dot_nt(q, k)
    m_bd = jnp.where(strict, -(bcol * kk * dec), 0.0)
    attn = qk * dec

    def pack(bd):
        return bd[0:CHUNK] + bd[CHUNK:2 * CHUNK] + bd[2 * CHUNK:3 * CHUNK] + bd[3 * CHUNK:]

    def unpack(pk):
        return jnp.where(same, jnp.concatenate([pk, pk, pk, pk], axis=0), 0.0)

    pi = lax.broadcasted_iota(jnp.int32, (CHUNK, GROUP), 0)
    pj = lax.broadcasted_iota(jnp.int32, (CHUNK, GROUP), 1)
    mk_pk = pack(m_bd)
    mk_bd = m_bd
    p = jnp.where(pi == (pj % CHUNK), 1.0, 0.0) + mk_pk
    for _ in range(5):
        mk_pk = jnp.dot(mk_pk, mk_bd, preferred_element_type=F32)
        mk_bd = unpack(mk_pk)
        p = p + jnp.dot(p, mk_bd, preferred_element_type=F32)
    t_bd = unpack(p)

    eg = jnp.exp(gam_col)
    rhs = jnp.concatenate([v * bcol, k * (bcol * eg)], axis=1)
    uw = jnp.dot(t_bd, rhs, preferred_element_type=F32)
    u = uw[:, :DN_HEAD_DIM]
    w = uw[:, DN_HEAD_DIM:]
    qg = q * eg
    kend = k * jnp.exp(tot_col - gam_col)
    gend = jnp.exp(tot_col)

    order = range(4) if dirn == 0 else range(3, -1, -1)
    zeros = jnp.zeros((CHUNK, DN_HEAD_DIM), F32)
    for c in order:
        lo = c * CHUNK
        wq = jnp.dot(jnp.concatenate([w[lo:lo + CHUNK], qg[lo:lo + CHUNK]], axis=0), state,
                     preferred_element_type=F32)
        v_new = u[lo:lo + CHUNK] - wq[:CHUNK]
        v_full = jnp.concatenate([v_new if cc == c else zeros for cc in range(4)], axis=0)
        o_c = wq[CHUNK:] + jnp.dot(attn[lo:lo + CHUNK], v_full, preferred_element_type=F32)
        o_ref[pl.ds(r0 + lo, CHUNK), :] = o_c
        state = state * gend[lo:lo + 1] + _dot_tn(kend[lo:lo + CHUNK], v_new)
    return state


def _dn_kernel(qp_ref, kp_ref, vp_ref, cwq_ref, cwk_ref, cwv_ref, gb_ref, gr_ref, s0f_ref, s0b_ref,
               of_ref, ob_ref, sf_ref, sb_ref, q_s, k_s, v_s, *, seq_len):
    ng = seq_len // GROUP
    _conv_act(qp_ref, cwq_ref, q_s, seq_len, True, DN_HEAD_DIM ** -0.5)
    _conv_act(kp_ref, cwk_ref, k_s, seq_len, True, 1.0)
    _conv_act(vp_ref, cwv_ref, v_s, seq_len, False, 1.0)

    def body(m, carry):
        s_f, s_b = carry
        s_f = _dn_group(0, m, s_f, q_s, k_s, v_s, gb_ref, gr_ref, of_ref)
        s_b = _dn_group(1, ng - 1 - m, s_b, q_s, k_s, v_s, gb_ref, gr_ref, ob_ref)
        return s_f, s_b

    s_f, s_b = lax.fori_loop(0, ng, body, (s0f_ref[...], s0b_ref[...]))
    sf_ref[...] = s_f
    sb_ref[...] = s_b


def _deltanet(qkv, conv_w, gbh, grow, s0f, s0b, bsz, seq_len):
    h, dh = DN_HEADS, DN_HEAD_DIM
    ng = seq_len // GROUP
    col = lambda off: (lambda b, hh: (b, off + hh))
    wcol = lambda off: (lambda b, hh: (0, off + hh))
    st = pl.BlockSpec((None, None, dh, dh), lambda b, hh: (b, hh, 0, 0))
    return pl.pallas_call(
        functools.partial(_dn_kernel, seq_len=seq_len),
        grid=(bsz, h),
        in_specs=[pl.BlockSpec((seq_len, dh), col(0)),
                  pl.BlockSpec((seq_len, dh), col(h)),
                  pl.BlockSpec((seq_len, dh), col(2 * h)),
                  pl.BlockSpec((CONV_K, dh), wcol(0)),
                  pl.BlockSpec((CONV_K, dh), wcol(h)),
                  pl.BlockSpec((CONV_K, dh), wcol(2 * h)),
                  pl.BlockSpec((None, None, seq_len, 4), lambda b, hh: (b, hh, 0, 0)),
                  pl.BlockSpec((None, None, 2, ng, GROUP), lambda b, hh: (b, hh, 0, 0, 0)),
                  st, st],
        out_specs=[pl.BlockSpec((seq_len, dh), col(0)),
                   pl.BlockSpec((seq_len, dh), col(0)),
                   st, st],
        out_shape=[jax.ShapeDtypeStruct((bsz * seq_len, h * dh), F32),
                   jax.ShapeDtypeStruct((bsz * seq_len, h * dh), F32),
                   jax.ShapeDtypeStruct((bsz, h, dh, dh), F32),
                   jax.ShapeDtypeStruct((bsz, h, dh, dh), F32)],
        scratch_shapes=[pltpu.VMEM((seq_len, dh), F32)] * 3,
        compiler_params=_cparams("parallel", "parallel"),
        name="deltanet",
    )(qkv, qkv, qkv, conv_w, conv_w, conv_w, gbh, grow, s0f, s0b)


def _dft_table_kernel(tac_ref, tas_ref, tbc_ref, tbs_ref, cp_ref, sp_ref, *, nt1):
    tbc = tbc_ref[...]
    tbs = tbs_ref[...]
    for t1 in range(nt1):
        ac = tac_ref[:, t1:t1 + 1]
        asn = tas_ref[:, t1:t1 + 1]
        cp_ref[:, t1 * LANES:(t1 + 1) * LANES] = (ac * tbc - asn * tbs).astype(BF16)
        sp_ref[:, t1 * LANES:(t1 + 1) * LANES] = (asn * tbc + ac * tbs).astype(BF16)


def _dft_tables(n):
    nt1 = n // LANES
    s = np.arange(n, dtype=np.int64)[:, None]
    t1 = np.arange(nt1, dtype=np.int64)[None, :]
    t0 = np.arange(LANES, dtype=np.int64)[None, :]
    ang_a = 2.0 * np.pi * ((s * t1) % nt1) / nt1
    ang_b = 2.0 * np.pi * ((s * t0) % n) / n
    pad = ((0, 0), (0, LANES - nt1))
    tac = jnp.asarray(np.pad(np.cos(ang_a), pad), F32)
    tas = jnp.asarray(np.pad(np.sin(ang_a), pad), F32)
    tbc = jnp.asarray(np.cos(ang_b), F32)
    tbs = jnp.asarray(np.sin(ang_b), F32)
    tr = min(n, 256)
    small = pl.BlockSpec((tr, LANES), lambda i: (i, 0))
    big = pl.BlockSpec((tr, n), lambda i: (i, 0))
    return pl.pallas_call(
        functools.partial(_dft_table_kernel, nt1=nt1),
        grid=(n // tr,),
        in_specs=[small, small, small, small],
        out_specs=[big, big],
        out_shape=[jax.ShapeDtypeStruct((n, n), BF16)] * 2,
        compiler_params=_cparams("parallel"),
        name="dft_tables",
    )(tac, tas, tbc, tbs)


def _fourier_kernel(f_ref, cp_ref, sp_ref, cc_ref, sc_ref, o_ref):
    kstep = pl.program_id(1)
    xb = f_ref[...].astype(BF16)
    cc = cc_ref[...]
    sc = sc_ref[...]
    zc = [jnp.dot(xb[:, g * LANES:(g + 1) * LANES], cc, preferred_element_type=F32) for g in range(FOURIER_GROUPS)]
    zs = [jnp.dot(xb[:, g * LANES:(g + 1) * LANES], sc, preferred_element_type=F32) for g in range(FOURIER_GROUPS)]
    zc = jnp.concatenate(zc, axis=1).astype(BF16)
    zs = jnp.concatenate(zs, axis=1).astype(BF16)
    part = (jnp.dot(cp_ref[...], zc, preferred_element_type=F32)
            - jnp.dot(sp_ref[...], zs, preferred_element_type=F32))

    @pl.when(kstep == 0)
    def _():
        o_ref[...] = part

    @pl.when(kstep > 0)
    def _():
        o_ref[...] += part


def _fourier(f, cp, sp, cc, sc, bsz, seq_len):
    fw = f.shape[1]
    tk = min(seq_len, 512)
    return pl.pallas_call(
        _fourier_kernel,
        grid=(bsz, seq_len // tk),
        in_specs=[pl.BlockSpec((tk, fw), lambda b, k: (b * (seq_len // tk) + k, 0)),
                  pl.BlockSpec((seq_len, tk), lambda b, k: (0, k)),
                  pl.BlockSpec((seq_len, tk), lambda b, k: (0, k)),
                  pl.BlockSpec((LANES, LANES), lambda b, k: (0, 0)),
                  pl.BlockSpec((LANES, LANES), lambda b, k: (0, 0))],
        out_specs=pl.BlockSpec((seq_len, fw), lambda b, k: (b, 0)),
        out_shape=jax.ShapeDtypeStruct((bsz * seq_len, fw), F32),
        compiler_params=_cparams("parallel", "arbitrary"),
        name="fourier",
    )(f, cp, sp, cc, sc)


def _outproj_kernel(four_ref, of_ref, ob_ref, z_ref, x_ref, gout_ref, gt1_ref, sc2_ref, sh2_ref, g2_ref,
                    wout_ref, wr_ref, br_ref, xn_ref, h2_ref, ti_ref, tg_ref, *, fw):
    o = of_ref[...] + ob_ref[...]
    z = z_ref[...]
    gout = gout_ref[...]
    parts = []
    for hh in range(DN_HEADS):
        sl = slice(hh * DN_HEAD_DIM, (hh + 1) * DN_HEAD_DIM)
        oh = o[:, sl]
        zh = z[:, sl]
        oh = oh * lax.rsqrt(jnp.mean(oh * oh, axis=-1, keepdims=True) + EPS) * gout
        parts.append(oh * (zh * _sigmoid(zh)))
    gated = jnp.concatenate(parts, axis=1).astype(BF16)
    mix = (jnp.dot(four_ref[...].astype(BF16), wout_ref[:fw, :], preferred_element_type=F32)
           + jnp.dot(gated, wout_ref[fw:, :], preferred_element_type=F32))
    xn = x_ref[...] + gt1_ref[...] * mix
    xn_ref[...] = xn
    inv = lax.rsqrt(jnp.mean(xn * xn, axis=-1, keepdims=True) + EPS)
    h2 = (xn * inv * g2_ref[...]) * (1.0 + sc2_ref[...]) + sh2_ref[...]
    tm = h2.shape[0]
    for s in range(h2.shape[1] // LANES):
        h2_ref[pl.ds(s, tm, stride=SUBLANES), :] = h2[:, s * LANES:(s + 1) * LANES]
    logits = jnp.dot(h2, wr_ref[...], preferred_element_type=F32, precision=lax.Precision.HIGHEST) + br_ref[...]
    lane = lax.broadcasted_iota(jnp.int32, logits.shape, 1)
    neg = jnp.float32(-jnp.inf)
    logits = jnp.where(lane < N_EXPERTS, logits, neg)
    vals, idxs = [], []
    for _ in range(TOP_K):
        m = jnp.max(logits, axis=-1, keepdims=True)
        idx = jnp.min(jnp.where(logits == m, lane, LANES), axis=-1, keepdims=True)
        vals.append(m)
        idxs.append(idx)
        logits = jnp.where(lane == idx, neg, logits)
    ex = [jnp.exp(vv - vals[0]) for vv in vals]
    den = ex[0] + ex[1] + ex[2] + ex[3]
    kl = lax.broadcasted_iota(jnp.int32, (tm, TOP_K), 1)
    ti = jnp.zeros((tm, TOP_K), jnp.int32)
    tg = jnp.zeros((tm, TOP_K), F32)
    for j in range(TOP_K):
        ti = jnp.where(kl == j, idxs[j], ti)
        tg = jnp.where(kl == j, ex[j] / den, tg)
    ti_ref[...] = ti
    tg_ref[...] = tg


def _outproj(four, o_f, o_b, z, x, g_out, gt1, sc2, sh2, g2, w_out, w_r, b_r, rows_per_group, tm=256):
    t, d = x.shape
    fw = four.shape[1]
    ng = gt1.shape[0]
    tpg = rows_per_group // tm
    grp = lambda i: (jnp.minimum(i // tpg, ng - 1), 0, 0)
    const = lambda i: (0, 0)
    row = lambda w: pl.BlockSpec((tm, w), lambda i: (i, 0))
    gspec = pl.BlockSpec((None, 1, d), grp)
    return pl.pallas_call(
        functools.partial(_outproj_kernel, fw=fw),
        grid=(t // tm,),
        in_specs=[row(fw), row(z.shape[1]), row(z.shape[1]), row(z.shape[1]), row(d),
                  pl.BlockSpec((1, DN_HEAD_DIM), const), gspec, gspec, gspec,
                  pl.BlockSpec((1, d), const),
                  pl.BlockSpec(w_out.shape, const),
                  pl.BlockSpec(w_r.shape, const),
                  pl.BlockSpec((1, LANES), const)],
        out_specs=[row(d),
                   pl.BlockSpec((tm * SUBLANES, LANES), lambda i: (i, 0)),
                   row(TOP_K), row(TOP_K)],
        out_shape=[jax.ShapeDtypeStruct((t, d), F32),
                   jax.ShapeDtypeStruct((t * d // LANES, LANES), F32),
                   jax.ShapeDtypeStruct((t, TOP_K), jnp.int32),
                   jax.ShapeDtypeStruct((t, TOP_K), F32)],
        compiler_params=_cparams("parallel"),
        name="outproj",
    )(four, o_f, o_b, z, x, g_out.reshape(1, DN_HEAD_DIM), gt1, sc2, sh2, g2.reshape(1, d), w_out, w_r, b_r)


def _row_copy(src_hbm, src_row, dst_vmem, dst_row, sem):
    return pltpu.make_async_copy(
        src_hbm.at[pl.ds(pl.multiple_of(src_row * SUBLANES, SUBLANES), SUBLANES), :],
        dst_vmem.at[pl.ds(pl.multiple_of(dst_row * SUBLANES, SUBLANES), SUBLANES), :], sem)


def _expert_kernel(be_ref, nu_ref, tok_ref, h_hbm, gate_ref, wg_ref, bg_ref, wu_ref, bu_ref, wd_ref, bd_ref,
                   y_ref, xbuf, x2d, sem):
    i = pl.program_id(0)
    rows = MOE_BLOCK
    nseg = x2d.shape[1] // LANES

    @pl.when(i < nu_ref[0])
    def _():
        def issue(r, carry):
            _row_copy(h_hbm, tok_ref[0, 0, r], xbuf, r, sem).start()
            return carry

        lax.fori_loop(0, rows, issue, 0)
        pltpu.make_async_copy(h_hbm.at[pl.ds(0, rows * SUBLANES), :], xbuf, sem).wait()
        for s in range(nseg):
            x2d[:, s * LANES:(s + 1) * LANES] = xbuf[pl.ds(s, rows, stride=SUBLANES), :].astype(BF16)
        x = x2d[...]
        a = jnp.minimum(jnp.dot(x, wg_ref[...], preferred_element_type=F32) + bg_ref[...], SWIGLU_LIMIT)
        u = jnp.clip(jnp.dot(x, wu_ref[...], preferred_element_type=F32) + bu_ref[...], -SWIGLU_LIMIT, SWIGLU_LIMIT)
        act = (a * _sigmoid(SWIGLU_ALPHA * a) * (u + 1.0)).astype(BF16)
        y = (jnp.dot(act, wd_ref[...], preferred_element_type=F32) + bd_ref[...]) * gate_ref[...]
        for s in range(nseg):
            y_ref[pl.ds(s, rows, stride=SUBLANES), :] = y[:, s * LANES:(s + 1) * LANES]

    @pl.when(i >= nu_ref[0])
    def _():
        y_ref[...] = jnp.zeros(y_ref.shape, F32)


def _experts(h_rt, slot_tok, slot_gate, block_e, n_used, w_gate, b_gate, w_up, b_up, w_down, b_down):
    n_blocks = block_e.shape[0]
    ne, d, ff = w_gate.shape
    seg = d // LANES
    wspec = lambda shp: pl.BlockSpec((None,) + shp, lambda i, be, nu: (be[i], 0, 0))
    grid_spec = pltpu.PrefetchScalarGridSpec(
        num_scalar_prefetch=2,
        grid=(n_blocks,),
        in_specs=[pl.BlockSpec((1, 1, MOE_BLOCK), lambda i, be, nu: (i, 0, 0), memory_space=pltpu.SMEM),
                  pl.BlockSpec(memory_space=pltpu.HBM),
                  pl.BlockSpec((MOE_BLOCK, 1), lambda i, be, nu: (i, 0)),
                  wspec((d, ff)), wspec((1, ff)), wspec((d, ff)), wspec((1, ff)), wspec((ff, d)), wspec((1, d))],
        out_specs=pl.BlockSpec((MOE_BLOCK * seg, LANES), lambda i, be, nu: (i, 0)),
        scratch_shapes=[pltpu.VMEM((MOE_BLOCK * seg, LANES), F32),
                        pltpu.VMEM((MOE_BLOCK, d), BF16),
                        pltpu.SemaphoreType.DMA(())],
    )
    return pl.pallas_call(
        _expert_kernel,
        grid_spec=grid_spec,
        out_shape=jax.ShapeDtypeStruct((n_blocks * MOE_BLOCK * seg, LANES), F32),
        compiler_params=_cparams("arbitrary"),
        name="experts",
    )(block_e, n_used, slot_tok.reshape(n_blocks, 1, MOE_BLOCK), h_rt, slot_gate.reshape(-1, 1),
      w_gate, b_gate.reshape(ne, 1, ff), w_up, b_up.reshape(ne, 1, ff), w_down, b_down.reshape(ne, 1, d))


def _combine_kernel(dest_ref, y_hbm, x_ref, gt_ref, o_ref, buf, sem, *, tm):
    n = tm * TOP_K

    def issue(e, carry):
        t = e // TOP_K
        j = e % TOP_K
        _row_copy(y_hbm, dest_ref[0, 0, e], buf, j * tm + t, sem).start()
        return carry

    lax.fori_loop(0, n, issue, 0)
    pltpu.make_async_copy(y_hbm.at[pl.ds(0, n * SUBLANES), :], buf, sem).wait()
    gt = gt_ref[...]
    for s in range(o_ref.shape[1] // LANES):
        acc = buf[pl.ds(s, tm, stride=SUBLANES), :]
        for j in range(1, TOP_K):
            acc = acc + buf[pl.ds(j * tm * SUBLANES + s, tm, stride=SUBLANES), :]
        sl = slice(s * LANES, (s + 1) * LANES)
        o_ref[:, sl] = x_ref[:, sl] + gt[:, sl] * acc


def _combine(x, y_rt, dest, gt2, rows_per_group, tm=128):
    t, d = x.shape
    ng = gt2.shape[0]
    tpg = rows_per_group // tm
    return pl.pallas_call(
        functools.partial(_combine_kernel, tm=tm),
        grid=(t // tm,),
        in_specs=[pl.BlockSpec((1, 1, tm * TOP_K), lambda i: (i, 0, 0), memory_space=pltpu.SMEM),
                  pl.BlockSpec(memory_space=pltpu.HBM),
                  pl.BlockSpec((tm, d), lambda i: (i, 0)),
                  pl.BlockSpec((None, 1, d), lambda i: (jnp.minimum(i // tpg, ng - 1), 0, 0))],
        out_specs=pl.BlockSpec((tm, d), lambda i: (i, 0)),
        out_shape=jax.ShapeDtypeStruct((t, d), F32),
        scratch_shapes=[pltpu.VMEM((tm * TOP_K * SUBLANES, LANES), F32), pltpu.SemaphoreType.DMA(())],
        compiler_params=_cparams("arbitrary"),
        name="combine",
    )(dest.reshape(t // tm, 1, tm * TOP_K), y_rt, x, gt2)


def _final_norm_kernel(x_ref, g_ref, o_ref):
    x = x_ref[...]
    o_ref[...] = x * lax.rsqrt(jnp.mean(x * x, axis=-1, keepdims=True) + EPS) * g_ref[...]


def _final_norm(x, g, tm=1024):
    t, d = x.shape
    return pl.pallas_call(
        _final_norm_kernel,
        grid=(t // tm,),
        in_specs=[pl.BlockSpec((tm, d), lambda i: (i, 0)), pl.BlockSpec((1, d), lambda i: (0, 0))],
        out_specs=pl.BlockSpec((tm, d), lambda i: (i, 0)),
        out_shape=jax.ShapeDtypeStruct((t, d), F32),
        compiler_params=_cparams("parallel"),
        name="final_norm",
    )(x, g.reshape(1, d))


def _route(top_i, top_g):
    t = top_i.shape[0]
    tk = t * TOP_K
    n_blocks = -(-tk // MOE_BLOCK) + N_EXPERTS
    p = n_blocks * MOE_BLOCK
    flat_e = top_i.reshape(tk)
    order = jnp.argsort(flat_e, stable=True).astype(jnp.int32)
    se = flat_e[order]
    counts = jnp.bincount(flat_e, length=N_EXPERTS).astype(jnp.int32)
    group_start = jnp.cumsum(counts) - counts
    padded = (counts + MOE_BLOCK - 1) // MOE_BLOCK * MOE_BLOCK
    padded_end = jnp.cumsum(padded)
    padded_start = padded_end - padded
    dest_sorted = (padded_start[se] + jnp.arange(tk, dtype=jnp.int32) - group_start[se]).astype(jnp.int32)
    slot_tok = jnp.zeros((p,), jnp.int32).at[dest_sorted].set(order // TOP_K)
    slot_gate = jnp.zeros((p,), F32).at[dest_sorted].set(top_g.reshape(tk)[order])
    dest = jnp.zeros((tk,), jnp.int32).at[order].set(dest_sorted)
    block_e = jnp.minimum(
        jnp.searchsorted(padded_end, jnp.arange(n_blocks, dtype=jnp.int32) * MOE_BLOCK, side='right'),
        N_EXPERTS - 1).astype(jnp.int32)
    n_used = (padded_end[-1:] // MOE_BLOCK).astype(jnp.int32)
    return slot_tok, slot_gate, dest.reshape(t, TOP_K), block_e, n_used


def _gate_tables(gb, bsz, seq_len):
    g5 = gb.reshape(bsz, seq_len, 2, 2, DN_HEADS)
    gbh = g5.transpose(0, 4, 1, 2, 3).reshape(bsz, DN_HEADS, seq_len, 4)
    grow = g5[:, :, :, 0, :].transpose(0, 3, 2, 1).reshape(bsz, DN_HEADS, 2, seq_len // GROUP, GROUP)
    return gbh, grow


def kernel(x, c, ctx, c_ctx, w_mod, b_mod, g_norm1, w_in, conv_w, a_log, dt_bias, g_out_norm, w_out, g_norm2,
           w_router, b_router, w_gate, b_gate, w_up, b_up, w_down, b_down, g_final):
    bsz, seq, d = x.shape
    clen = ctx.shape[1]
    depth = w_mod.shape[0]
    fw = FOURIER_GROUPS * LANES
    nmain = fw + 4 * DN_HEADS * DN_HEAD_DIM
    nab = 4 * DN_HEADS

    xt = x.reshape(bsz * seq, d)
    ct = ctx.reshape(bsz * clen, d)
    c_rows = jnp.concatenate([c, c_ctx[None, :], jnp.zeros((2 * SUBLANES - bsz - 1, d), F32)], axis=0)

    cp_x, sp_x = _dft_tables(seq)
    cp_c, sp_c = _dft_tables(clen)
    ch = np.arange(LANES, dtype=np.int64)
    ang = 2.0 * np.pi * ((ch[:, None] * ch[None, :]) % LANES) / LANES

    def chan(tab, n):
        return jnp.asarray(tab / math.sqrt(n * LANES), F32).astype(BF16)

    lanepad = lambda v: jnp.pad(v, ((0, 0), (0, LANES - v.shape[1])))
    zeros4 = jnp.zeros((DN_HEADS,), F32)
    isdec_row = lanepad(jnp.tile(jnp.concatenate([jnp.ones((DN_HEADS,), F32), zeros4]), 2)[None, :])

    for l in range(depth):
        update_ctx = l < depth - 1
        mod = _mod(c_rows, w_mod[l], b_mod[l]).reshape(2 * SUBLANES, 6, 1, d)
        mx = lambda j: mod[:bsz, j]
        mc = lambda j: mod[bsz:bsz + 1, j]

        w_main = w_in[l][:, :nmain].astype(BF16)
        w_ab = lanepad(w_in[l][:, nmain:]).astype(BF16)
        alog_row = lanepad(jnp.concatenate([a_log[l][0], zeros4, a_log[l][1], zeros4])[None, :])
        dtb_row = lanepad(jnp.concatenate([dt_bias[l][0], zeros4, dt_bias[l][1], zeros4])[None, :])
        ip = lambda tok, sh, sc, rpg: _inproj(tok, g_norm1[l], sc, sh, w_main, w_ab, alog_row, dtb_row,
                                               isdec_row, rpg)
        f_x, qkv_x, z_x, gb_x = ip(xt, mx(0), mx(1), seq)
        f_c, qkv_c, z_c, gb_c = ip(ct, mc(0), mc(1), bsz * clen)

        gbh_c, grow_c = _gate_tables(gb_c, bsz, clen)
        gbh_x, grow_x = _gate_tables(gb_x, bsz, seq)
        s_zero = jnp.zeros((bsz, DN_HEADS, DN_HEAD_DIM, DN_HEAD_DIM), F32)
        ocf, ocb, s_f, s_b = _deltanet(qkv_c, conv_w[l], gbh_c, grow_c, s_zero, s_zero, bsz, clen)
        oxf, oxb, _, _ = _deltanet(qkv_x, conv_w[l], gbh_x, grow_x, s_f, s_b, bsz, seq)

        w_out_b = w_out[l].astype(BF16)
        w_r = lanepad(w_router[l])
        b_r = lanepad(b_router[l][None, :])
        four_x = _fourier(f_x, cp_x, sp_x, chan(np.cos(ang), seq), chan(np.sin(ang), seq), bsz, seq)
        xn_x, h2_x, ti_x, tg_x = _outproj(four_x, oxf, oxb, z_x, xt, g_out_norm[l], mx(2), mx(4), mx(3),
                                           g_norm2[l], w_out_b, w_r, b_r, seq)
        if update_ctx:
            four_c = _fourier(f_c, cp_c, sp_c, chan(np.cos(ang), clen), chan(np.sin(ang), clen), bsz, clen)
            xn_c, h2_c, ti_c, tg_c = _outproj(four_c, ocf, ocb, z_c, ct, g_out_norm[l], mc(2), mc(4), mc(3),
                                               g_norm2[l], w_out_b, w_r, b_r, bsz * clen)
            h2 = jnp.concatenate([h2_x, h2_c], axis=0)
            ti = jnp.concatenate([ti_x, ti_c], axis=0)
            tg = jnp.concatenate([tg_x, tg_c], axis=0)
        else:
            h2, ti, tg = h2_x, ti_x, tg_x

        slot_tok, slot_gate, dest, block_e, n_used = _route(ti, tg)
        y_rt = _experts(h2, slot_tok, slot_gate, block_e, n_used,
                        w_gate[l].astype(BF16), b_gate[l], w_up[l].astype(BF16), b_up[l],
                        w_down[l].astype(BF16), b_down[l])
        xt = _combine(xn_x, y_rt, dest[:bsz * seq], mx(5), seq)
        if update_ctx:
            ct = _combine(xn_c, y_rt, dest[bsz * seq:], mc(5), bsz * clen)

    return _final_norm(xt, g_final).reshape(bsz, seq, d)
```

```python
import functools
import math

import numpy as np
import jax
import jax.numpy as jnp
from jax import lax
from jax.experimental import pallas as pl
from jax.experimental.pallas import tpu as pltpu

FOURIER_GROUPS = 4
DN_HEADS = 4
DN_HEAD_DIM = 128
CONV_K = 5
N_EXPERTS = 32
TOP_K = 4
SWIGLU_LIMIT = 7.0
SWIGLU_ALPHA = 1.702
EPS = 1e-6

LANES = 128
SUBLANES = 8
V7X_VMEM_BYTES = 64 * 1024 * 1024
VMEM_LIMIT = V7X_VMEM_BYTES - 8 * 1024 * 1024
CHUNK = 64
GROUP = 4 * CHUNK
MOE_BLOCK = 256

F32 = jnp.float32
BF16 = jnp.bfloat16


def _cparams(*sem):
    return pltpu.CompilerParams(dimension_semantics=sem, vmem_limit_bytes=VMEM_LIMIT)


def _sigmoid(x):
    return 1.0 / (1.0 + jnp.exp(-x))


def _mod_kernel(c_ref, w_ref, b_ref, o_ref):
    c = c_ref[...]
    a = c * _sigmoid(c)
    o_ref[...] = jnp.dot(a, w_ref[...], preferred_element_type=F32) + b_ref[...]


def _mod(c_rows, w_mod, b_mod):
    r, d = c_rows.shape
    n = w_mod.shape[1]
    tn = n // 4
    return pl.pallas_call(
        _mod_kernel,
        grid=(n // tn,),
        in_specs=[pl.BlockSpec((r, d), lambda j: (0, 0)),
                  pl.BlockSpec((d, tn), lambda j: (0, j)),
                  pl.BlockSpec((1, tn), lambda j: (0, j))],
        out_specs=pl.BlockSpec((r, tn), lambda j: (0, j)),
        out_shape=jax.ShapeDtypeStruct((r, n), F32),
        compiler_params=_cparams("parallel"),
        name="mod",
    )(c_rows, w_mod, b_mod.reshape(1, n))


def _inproj_kernel(x_ref, g_ref, sc_ref, sh_ref, w_ref, wab_ref, alog_ref, dtb_ref, isdec_ref,
                   f_ref, qkv_ref, z_ref, gb_ref, *, fw, qw):
    x = x_ref[...]
    inv = lax.rsqrt(jnp.mean(x * x, axis=-1, keepdims=True) + EPS)
    h = (x * inv * g_ref[...]) * (1.0 + sc_ref[...]) + sh_ref[...]
    hb = h.astype(BF16)
    f_ref[...] = jnp.dot(hb, w_ref[:, :fw], preferred_element_type=F32)
    for j in range(3):
        lo = fw + j * qw
        qkv_ref[:, j * qw:(j + 1) * qw] = jnp.dot(hb, w_ref[:, lo:lo + qw], preferred_element_type=F32)
    lo = fw + 3 * qw
    z_ref[...] = jnp.dot(hb, w_ref[:, lo:lo + qw], preferred_element_type=F32)
    ab = jnp.dot(hb, wab_ref[...], preferred_element_type=F32)
    pre = ab + dtb_ref[...]
    softplus = jnp.maximum(pre, 0.0) + jnp.log(1.0 + jnp.exp(-jnp.abs(pre)))
    dec = -jnp.exp(alog_ref[...]) * softplus
    gb = jnp.where(isdec_ref[...] > 0.0, dec, _sigmoid(ab))
    gb_ref[...] = gb[:, :gb_ref.shape[1]]


def _inproj(x, g1, sc, sh, w_main, w_ab, alog_row, dtb_row, isdec_row, rows_per_group, tm=512):
    t, d = x.shape
    ng = sc.shape[0]
    fw = FOURIER_GROUPS * LANES
    qw = DN_HEADS * DN_HEAD_DIM
    nab = 4 * DN_HEADS
    tpg = rows_per_group // tm
    grp = lambda i: (jnp.minimum(i // tpg, ng - 1), 0, 0)
    const = lambda i: (0, 0)
    return pl.pallas_call(
        functools.partial(_inproj_kernel, fw=fw, qw=qw),
        grid=(t // tm,),
        in_specs=[pl.BlockSpec((tm, d), lambda i: (i, 0)),
                  pl.BlockSpec((1, d), const),
                  pl.BlockSpec((None, 1, d), grp),
                  pl.BlockSpec((None, 1, d), grp),
                  pl.BlockSpec(w_main.shape, const),
                  pl.BlockSpec(w_ab.shape, const),
                  pl.BlockSpec((1, LANES), const),
                  pl.BlockSpec((1, LANES), const),
                  pl.BlockSpec((1, LANES), const)],
        out_specs=[pl.BlockSpec((tm, fw), lambda i: (i, 0)),
                   pl.BlockSpec((tm, 3 * qw), lambda i: (i, 0)),
                   pl.BlockSpec((tm, qw), lambda i: (i, 0)),
                   pl.BlockSpec((tm, nab), lambda i: (i, 0))],
        out_shape=[jax.ShapeDtypeStruct((t, fw), F32),
                   jax.ShapeDtypeStruct((t, 3 * qw), F32),
                   jax.ShapeDtypeStruct((t, qw), F32),
                   jax.ShapeDtypeStruct((t, nab), F32)],
        compiler_params=_cparams("parallel"),
        name="inproj",
    )(x, g1.reshape(1, d), sc, sh, w_main, w_ab, alog_row, dtb_row, isdec_row)


def _conv_act(src_ref, w_ref, dst_ref, seq_len, l2norm, scale):
    r = GROUP
    nt = seq_len // r
    halo = SUBLANES
    w = w_ref[...]

    def body(i, carry):
        r0 = pl.multiple_of(i * r, r)
        cur = src_ref[pl.ds(r0, r), :]
        p0 = pl.multiple_of(jnp.maximum(r0 - halo, 0), halo)
        n0 = pl.multiple_of(jnp.minimum(r0 + r, seq_len - halo), halo)
        prev = jnp.where(i > 0, src_ref[pl.ds(p0, halo), :], 0.0)
        nxt = jnp.where(i < nt - 1, src_ref[pl.ds(n0, halo), :], 0.0)
        win = jnp.concatenate([prev, cur, nxt], axis=0)
        n = r + 2 * halo
        acc = None
        for j in range(CONV_K):
            off = halo - CONV_K // 2 + j
            sh = pltpu.roll(win, n - off, 0)[:r]
            term = w[j:j + 1, :] * sh
            acc = term if acc is None else acc + term
        y = acc * _sigmoid(acc)
        if l2norm:
            segs = []
            for hh in range(y.shape[1] // DN_HEAD_DIM):
                ys = y[:, hh * DN_HEAD_DIM:(hh + 1) * DN_HEAD_DIM]
                segs.append(ys * (lax.rsqrt(jnp.sum(ys * ys, axis=-1, keepdims=True) + EPS) * scale))
            y = jnp.concatenate(segs, axis=1)
        dst_ref[pl.ds(r0, r), :] = y
        return carry

    lax.fori_loop(0, nt, body, 0)


def _dot_nt(a, b):
    return lax.dot_general(a, b, (((1,), (1,)), ((), ())), preferred_element_type=F32)


def _dot_tn(a, b):
    return lax.dot_general(a, b, (((0,), (0,)), ((), ())), preferred_element_type=F32)


def _dn_masks(dirn):
    ii = lax.broadcasted_iota(jnp.int32, (GROUP, GROUP), 0)
    jj = lax.broadcasted_iota(jnp.int32, (GROUP, GROUP), 1)
    same = (ii // CHUNK) == (jj // CHUNK)
    if dirn == 0:
        return same, same & (ii >= jj), same & (ii > jj), same & (ii <= jj)
    return same, same & (ii <= jj), same & (ii < jj), same & (ii >= jj)


def _dn_groups(chains, states, q_s, k_s, v_s, gb_ref, gr_ref, o_refs):
    rng = range(len(chains))
    dh = DN_HEAD_DIM
    masks = {d: _dn_masks(d) for d in sorted({c[1] for c in chains})}
    same = [masks[c[1]][0] for c in chains]

    def pack(bd):
        return bd[0:CHUNK] + bd[CHUNK:2 * CHUNK] + bd[2 * CHUNK:3 * CHUNK] + bd[3 * CHUNK:]

    def unpack(pk, sm):
        return jnp.where(sm, jnp.concatenate([pk, pk, pk, pk], axis=0), 0.0)

    r0s, q, k, v, bcol, gam_col, tot_col, dec = [], [], [], [], [], [], [], []
    for hh, dirn, g in chains:
        r0 = pl.multiple_of(g * GROUP, GROUP)
        cols = slice(hh * dh, (hh + 1) * dh)
        r0s.append(r0)
        q.append(q_s[pl.ds(r0, GROUP), cols])
        k.append(k_s[pl.ds(r0, GROUP), cols])
        v.append(v_s[pl.ds(r0, GROUP), cols])
        gcol = gb_ref[hh, pl.ds(r0, GROUP), 2 * dirn:2 * dirn + 1]
        bcol.append(gb_ref[hh, pl.ds(r0, GROUP), 2 * dirn + 1:2 * dirn + 2])
        grow = gr_ref[hh, dirn, pl.ds(g, 1), :]
        sm, incl, _, incl_t = masks[dirn]
        gc = jnp.sum(jnp.where(incl, grow, 0.0), axis=1, keepdims=True)
        gr = jnp.sum(jnp.where(incl_t, gcol, 0.0), axis=0, keepdims=True)
        gam_col.append(gc)
        tot_col.append(jnp.sum(jnp.where(sm, grow, 0.0), axis=1, keepdims=True))
        dec.append(jnp.where(incl, jnp.exp(jnp.where(incl, gc - gr, 0.0)), 0.0))

    kk = [_dot_nt(k[i], k[i]) for i in rng]
    qk = [_dot_nt(q[i], k[i]) for i in rng]
    mk_bd = [jnp.where(masks[chains[i][1]][2], -(bcol[i] * kk[i] * dec[i]), 0.0) for i in rng]
    attn = [qk[i] * dec[i] for i in rng]

    pi = lax.broadcasted_iota(jnp.int32, (CHUNK, GROUP), 0)
    pj = lax.broadcasted_iota(jnp.int32, (CHUNK, GROUP), 1)
    eye_pk = jnp.where(pi == (pj % CHUNK), 1.0, 0.0)
    mk_pk = [pack(mk_bd[i]) for i in rng]
    p = [eye_pk + mk_pk[i] for i in rng]
    for _ in range(5):
        mk_pk = [jnp.dot(mk_pk[i], mk_bd[i], preferred_element_type=F32) for i in rng]
        mk_bd = [unpack(mk_pk[i], same[i]) for i in rng]
        p = [p[i] + jnp.dot(p[i], mk_bd[i], preferred_element_type=F32) for i in rng]
    t_bd = [unpack(p[i], same[i]) for i in rng]

    eg = [jnp.exp(gam_col[i]) for i in rng]
    rhs = [jnp.concatenate([v[i] * bcol[i], k[i] * (bcol[i] * eg[i])], axis=1) for i in rng]
    uw = [jnp.dot(t_bd[i], rhs[i], preferred_element_type=F32) for i in rng]
    qg = [q[i] * eg[i] for i in rng]
    kend = [k[i] * jnp.exp(tot_col[i] - gam_col[i]) for i in rng]
    gend = [jnp.exp(tot_col[i]) for i in rng]

    zeros = jnp.zeros((CHUNK, dh), F32)
    for step in range(GROUP // CHUNK):
        cs = [step if chains[i][1] == 0 else GROUP // CHUNK - 1 - step for i in rng]
        lo = [c * CHUNK for c in cs]
        lhs = [jnp.concatenate([uw[i][lo[i]:lo[i] + CHUNK, dh:], qg[i][lo[i]:lo[i] + CHUNK]], axis=0)
               for i in rng]
        wq = [jnp.dot(lhs[i], states[i], preferred_element_type=F32) for i in rng]
        v_new = [uw[i][lo[i]:lo[i] + CHUNK, :dh] - wq[i][:CHUNK] for i in rng]
        states = [states[i] * gend[i][lo[i]:lo[i] + 1] + _dot_tn(kend[i][lo[i]:lo[i] + CHUNK], v_new[i])
                  for i in rng]
        for i in rng:
            v_full = jnp.concatenate([v_new[i] if cc == cs[i] else zeros for cc in range(GROUP // CHUNK)], axis=0)
            o_c = wq[i][CHUNK:] + jnp.dot(attn[i][lo[i]:lo[i] + CHUNK], v_full, preferred_element_type=F32)
            hh = chains[i][0]
            o_refs[chains[i][1]][pl.ds(r0s[i] + lo[i], CHUNK), hh * dh:(hh + 1) * dh] = o_c
    return states


def _dn_kernel(qp_ref, kp_ref, vp_ref, cwq_ref, cwk_ref, cwv_ref, gb_ref, gr_ref, s0f_ref, s0b_ref,
               of_ref, ob_ref, sf_ref, sb_ref, q_s, k_s, v_s, *, seq_len, nh):
    ng = seq_len // GROUP
    _conv_act(qp_ref, cwq_ref, q_s, seq_len, True, DN_HEAD_DIM ** -0.5)
    _conv_act(kp_ref, cwk_ref, k_s, seq_len, True, 1.0)
    _conv_act(vp_ref, cwv_ref, v_s, seq_len, False, 1.0)

    def body(m, carry):
        chains = [(hh, d, m if d == 0 else ng - 1 - m) for hh in range(nh) for d in (0, 1)]
        return tuple(_dn_groups(chains, list(carry), q_s, k_s, v_s, gb_ref, gr_ref, (of_ref, ob_ref)))

    init = tuple(r[hh] for hh in range(nh) for r in (s0f_ref, s0b_ref))
    fin = lax.fori_loop(0, ng, body, init)
    for hh in range(nh):
        sf_ref[hh] = fin[2 * hh]
        sb_ref[hh] = fin[2 * hh + 1]


def _deltanet(qkv, conv_w, gbh, grow, s0f, s0b, bsz, seq_len, nh=2):
    h, dh = DN_HEADS, DN_HEAD_DIM
    ng = seq_len // GROUP
    nhb = h // nh
    w = nh * dh
    once = pl.Buffered(1)
    col = lambda off: (lambda b, hb: (b, off + hb))
    wcol = lambda off: (lambda b, hb: (0, off + hb))
    st = pl.BlockSpec((None, nh, dh, dh), lambda b, hb: (b, hb, 0, 0))
    return pl.pallas_call(
        functools.partial(_dn_kernel, seq_len=seq_len, nh=nh),
        grid=(bsz, nhb),
        in_specs=[pl.BlockSpec((seq_len, w), col(0), pipeline_mode=once),
                  pl.BlockSpec((seq_len, w), col(nhb), pipeline_mode=once),
                  pl.BlockSpec((seq_len, w), col(2 * nhb), pipeline_mode=once),
                  pl.BlockSpec((CONV_K, w), wcol(0)),
                  pl.BlockSpec((CONV_K, w), wcol(nhb)),
                  pl.BlockSpec((CONV_K, w), wcol(2 * nhb)),
                  pl.BlockSpec((None, nh, seq_len, 4), lambda b, hb: (b, hb, 0, 0), pipeline_mode=once),
                  pl.BlockSpec((None, nh, 2, ng, GROUP), lambda b, hb: (b, hb, 0, 0, 0)),
                  st, st],
        out_specs=[pl.BlockSpec((seq_len, w), col(0)),
                   pl.BlockSpec((seq_len, w), col(0)),
                   st, st],
        out_shape=[jax.ShapeDtypeStruct((bsz * seq_len, h * dh), F32),
                   jax.ShapeDtypeStruct((bsz * seq_len, h * dh), F32),
                   jax.ShapeDtypeStruct((bsz, h, dh, dh), F32),
                   jax.ShapeDtypeStruct((bsz, h, dh, dh), F32)],
        scratch_shapes=[pltpu.VMEM((seq_len, w), F32)] * 3,
        compiler_params=_cparams("parallel", "parallel"),
        name="deltanet",
    )(qkv, qkv, qkv, conv_w, conv_w, conv_w, gbh, grow, s0f, s0b)


def _dft_table_kernel(tac_ref, tas_ref, tbc_ref, tbs_ref, cp_ref, sp_ref, *, nt1):
    tbc = tbc_ref[...]
    tbs = tbs_ref[...]
    for t1 in range(nt1):
        ac = tac_ref[:, t1:t1 + 1]
        asn = tas_ref[:, t1:t1 + 1]
        cp_ref[:, t1 * LANES:(t1 + 1) * LANES] = (ac * tbc - asn * tbs).astype(BF16)
        sp_ref[:, t1 * LANES:(t1 + 1) * LANES] = (asn * tbc + ac * tbs).astype(BF16)


def _dft_tables(n):
    nt1 = n // LANES
    s = np.arange(n, dtype=np.int64)[:, None]
    t1 = np.arange(nt1, dtype=np.int64)[None, :]
    t0 = np.arange(LANES, dtype=np.int64)[None, :]
    ang_a = 2.0 * np.pi * ((s * t1) % nt1) / nt1
    ang_b = 2.0 * np.pi * ((s * t0) % n) / n
    pad = ((0, 0), (0, LANES - nt1))
    tac = jnp.asarray(np.pad(np.cos(ang_a), pad), F32)
    tas = jnp.asarray(np.pad(np.sin(ang_a), pad), F32)
    tbc = jnp.asarray(np.cos(ang_b), F32)
    tbs = jnp.asarray(np.sin(ang_b), F32)
    tr = min(n, 256)
    small = pl.BlockSpec((tr, LANES), lambda i: (i, 0))
    big = pl.BlockSpec((tr, n), lambda i: (i, 0))
    return pl.pallas_call(
        functools.partial(_dft_table_kernel, nt1=nt1),
        grid=(n // tr,),
        in_specs=[small, small, small, small],
        out_specs=[big, big],
        out_shape=[jax.ShapeDtypeStruct((n, n), BF16)] * 2,
        compiler_params=_cparams("parallel"),
        name="dft_tables",
    )(tac, tas, tbc, tbs)


def _fourier_kernel(f_ref, cp_ref, sp_ref, cc_ref, sc_ref, o_ref):
    kstep = pl.program_id(1)
    xb = f_ref[...].astype(BF16)
    cc = cc_ref[...]
    sc = sc_ref[...]
    zc = [jnp.dot(xb[:, g * LANES:(g + 1) * LANES], cc, preferred_element_type=F32) for g in range(FOURIER_GROUPS)]
    zs = [jnp.dot(xb[:, g * LANES:(g + 1) * LANES], sc, preferred_element_type=F32) for g in range(FOURIER_GROUPS)]
    zc = jnp.concatenate(zc, axis=1).astype(BF16)
    zs = jnp.concatenate(zs, axis=1).astype(BF16)
    part = (jnp.dot(cp_ref[...], zc, preferred_element_type=F32)
            - jnp.dot(sp_ref[...], zs, preferred_element_type=F32))

    @pl.when(kstep == 0)
    def _():
        o_ref[...] = part

    @pl.when(kstep > 0)
    def _():
        o_ref[...] += part


def _fourier(f, cp, sp, cc, sc, bsz, seq_len):
    fw = f.shape[1]
    tk = min(seq_len, 512)
    return pl.pallas_call(
        _fourier_kernel,
        grid=(bsz, seq_len // tk),
        in_specs=[pl.BlockSpec((tk, fw), lambda b, k: (b * (seq_len // tk) + k, 0)),
                  pl.BlockSpec((seq_len, tk), lambda b, k: (0, k)),
                  pl.BlockSpec((seq_len, tk), lambda b, k: (0, k)),
                  pl.BlockSpec((LANES, LANES), lambda b, k: (0, 0)),
                  pl.BlockSpec((LANES, LANES), lambda b, k: (0, 0))],
        out_specs=pl.BlockSpec((seq_len, fw), lambda b, k: (b, 0)),
        out_shape=jax.ShapeDtypeStruct((bsz * seq_len, fw), F32),
        compiler_params=_cparams("parallel", "arbitrary"),
        name="fourier",
    )(f, cp, sp, cc, sc)


def _outproj_kernel(four_ref, of_ref, ob_ref, z_ref, x_ref, gout_ref, gt1_ref, sc2_ref, sh2_ref, g2_ref,
                    wout_ref, wr_ref, br_ref, cnt0_ref, xn_ref, h2_ref, ti_ref, tg_ref, rk_ref, cnt_ref,
                    carry, *, fw):
    @pl.when(pl.program_id(0) == 0)
    def _():
        carry[...] = cnt0_ref[...]

    o = of_ref[...] + ob_ref[...]
    z = z_ref[...]
    gout = gout_ref[...]
    parts = []
    for hh in range(DN_HEADS):
        sl = slice(hh * DN_HEAD_DIM, (hh + 1) * DN_HEAD_DIM)
        oh = o[:, sl]
        zh = z[:, sl]
        oh = oh * lax.rsqrt(jnp.mean(oh * oh, axis=-1, keepdims=True) + EPS) * gout
        parts.append(oh * (zh * _sigmoid(zh)))
    gated = jnp.concatenate(parts, axis=1).astype(BF16)
    mix = (jnp.dot(four_ref[...].astype(BF16), wout_ref[:fw, :], preferred_element_type=F32)
           + jnp.dot(gated, wout_ref[fw:, :], preferred_element_type=F32))
    xn = x_ref[...] + gt1_ref[...] * mix
    xn_ref[...] = xn
    inv = lax.rsqrt(jnp.mean(xn * xn, axis=-1, keepdims=True) + EPS)
    h2 = (xn * inv * g2_ref[...]) * (1.0 + sc2_ref[...]) + sh2_ref[...]
    tm = h2.shape[0]
    for s in range(h2.shape[1] // LANES):
        h2_ref[pl.ds(s, tm, stride=SUBLANES), :] = h2[:, s * LANES:(s + 1) * LANES]
    logits = jnp.dot(h2, wr_ref[...], preferred_element_type=F32, precision=lax.Precision.HIGHEST) + br_ref[...]
    lane = lax.broadcasted_iota(jnp.int32, logits.shape, 1)
    neg = jnp.float32(-jnp.inf)
    logits = jnp.where(lane < N_EXPERTS, logits, neg)
    vals, idxs = [], []
    for _ in range(TOP_K):
        m = jnp.max(logits, axis=-1, keepdims=True)
        idx = jnp.min(jnp.where(logits == m, lane, LANES), axis=-1, keepdims=True)
        vals.append(m)
        idxs.append(idx)
        logits = jnp.where(lane == idx, neg, logits)
    ex = [jnp.exp(vv - vals[0]) for vv in vals]
    den = ex[0] + ex[1] + ex[2] + ex[3]
    onehot = jnp.zeros(logits.shape, F32)
    for j in range(TOP_K):
        onehot = onehot + jnp.where(lane == idxs[j], 1.0, 0.0)
    ri = lax.broadcasted_iota(jnp.int32, (tm, tm), 0)
    ci = lax.broadcasted_iota(jnp.int32, (tm, tm), 1)
    tri = jnp.where(ri > ci, 1.0, 0.0).astype(BF16)
    base = jnp.dot(tri, onehot.astype(BF16), preferred_element_type=F32) + carry[...]
    carry[...] = carry[...] + jnp.sum(onehot, axis=0, keepdims=True)
    cnt_ref[...] = carry[...]
    kl = lax.broadcasted_iota(jnp.int32, (tm, TOP_K), 1)
    ti = jnp.zeros((tm, TOP_K), jnp.int32)
    tg = jnp.zeros((tm, TOP_K), F32)
    rk = jnp.zeros((tm, TOP_K), jnp.int32)
    for j in range(TOP_K):
        rank_j = jnp.sum(jnp.where(lane == idxs[j], base, 0.0), axis=-1, keepdims=True).astype(jnp.int32)
        ti = jnp.where(kl == j, idxs[j], ti)
        tg = jnp.where(kl == j, ex[j] / den, tg)
        rk = jnp.where(kl == j, rank_j, rk)
    ti_ref[...] = ti
    tg_ref[...] = tg
    rk_ref[...] = rk


def _outproj(four, o_f, o_b, z, x, g_out, gt1, sc2, sh2, g2, w_out, w_r, b_r, cnt0, rows_per_group, tm=256):
    t, d = x.shape
    fw = four.shape[1]
    ng = gt1.shape[0]
    tpg = rows_per_group // tm
    grp = lambda i: (jnp.minimum(i // tpg, ng - 1), 0, 0)
    const = lambda i: (0, 0)
    row = lambda w: pl.BlockSpec((tm, w), lambda i: (i, 0))
    gspec = pl.BlockSpec((None, 1, d), grp)
    return pl.pallas_call(
        functools.partial(_outproj_kernel, fw=fw),
        grid=(t // tm,),
        in_specs=[row(fw), row(z.shape[1]), row(z.shape[1]), row(z.shape[1]), row(d),
                  pl.BlockSpec((1, DN_HEAD_DIM), const), gspec, gspec, gspec,
                  pl.BlockSpec((1, d), const),
                  pl.BlockSpec(w_out.shape, const),
                  pl.BlockSpec(w_r.shape, const),
                  pl.BlockSpec((1, LANES), const),
                  pl.BlockSpec((1, LANES), const)],
        out_specs=[row(d),
                   pl.BlockSpec((tm * SUBLANES, LANES), lambda i: (i, 0)),
                   row(TOP_K), row(TOP_K), row(TOP_K),
                   pl.BlockSpec((1, LANES), const)],
        out_shape=[jax.ShapeDtypeStruct((t, d), F32),
                   jax.ShapeDtypeStruct((t * d // LANES, LANES), F32),
                   jax.ShapeDtypeStruct((t, TOP_K), jnp.int32),
                   jax.ShapeDtypeStruct((t, TOP_K), F32),
                   jax.ShapeDtypeStruct((t, TOP_K), jnp.int32),
                   jax.ShapeDtypeStruct((1, LANES), F32)],
        scratch_shapes=[pltpu.VMEM((1, LANES), F32)],
        compiler_params=_cparams("arbitrary"),
        name="outproj",
    )(four, o_f, o_b, z, x, g_out.reshape(1, DN_HEAD_DIM), gt1, sc2, sh2, g2.reshape(1, d), w_out, w_r, b_r, cnt0)


TOK_UNROLL = 4


def _row_copy(src, src_row, dst, dst_row, sem):
    return pltpu.make_async_copy(
        src.at[pl.ds(pl.multiple_of(src_row * SUBLANES, SUBLANES), SUBLANES), :],
        dst.at[pl.ds(pl.multiple_of(dst_row * SUBLANES, SUBLANES), SUBLANES), :], sem)


def _dispatch_kernel(cnt_ref, ps_ref, pd_ref, nu_ref, *refs, tm, steps, n_blocks):
    ns = len(steps)
    dest_refs, h_refs = refs[0:2 * ns:2], refs[1:2 * ns:2]
    xs_hbm, zero_s, sem, sem_pad = refs[2 * ns:]
    i = pl.program_id(0)
    blk_rows = MOE_BLOCK * SUBLANES

    def tail_copy(b):
        return pltpu.make_async_copy(zero_s, xs_hbm.at[pl.ds(pl.multiple_of(b * blk_rows, blk_rows), blk_rows), :],
                                     sem_pad)

    @pl.when(i == 0)
    def _():
        zero_s[...] = jnp.zeros(zero_s.shape, F32)

        def per_expert(e, tot):
            def one(r, c):
                _row_copy(zero_s, 0, xs_hbm, ps_ref[e] + r, sem_pad).start()
                return c

            lax.fori_loop(cnt_ref[e], pd_ref[e], one, 0)
            return tot + pd_ref[e] - cnt_ref[e]

        n_pad = lax.fori_loop(0, N_EXPERTS, per_expert, 0)

        def start_tail(b, c):
            tail_copy(b).start()
            return c

        lax.fori_loop(nu_ref[0], n_blocks, start_tail, 0)

        def wait_row(r, c):
            _row_copy(zero_s, 0, xs_hbm, 0, sem_pad).wait()
            return c

        lax.fori_loop(0, n_pad, wait_row, 0)

        def wait_tail(b, c):
            tail_copy(b).wait()
            return c

        lax.fori_loop(nu_ref[0], n_blocks, wait_tail, 0)

    first = 0
    for dest_ref, h_ref, nst in zip(dest_refs, h_refs, steps):
        @pl.when((i >= first) & (i < first + nst))
        def _(dest_ref=dest_ref, h_ref=h_ref):
            def issue(blk, carry):
                for tt in range(TOK_UNROLL):
                    t = blk * TOK_UNROLL + tt
                    for j in range(TOP_K):
                        _row_copy(h_ref, t, xs_hbm, dest_ref[0, 0, t * TOP_K + j], sem).start()
                return carry

            lax.fori_loop(0, tm // TOK_UNROLL, issue, 0)
            for _ in range(TOP_K):
                pltpu.make_async_copy(h_ref, xs_hbm.at[pl.ds(0, tm * SUBLANES), :], sem).wait()

        first += nst


def _dispatch(streams, counts, pstart, padded, n_used, n_blocks, tm=256):
    seg = streams[0][0].shape[0] // streams[0][1].shape[0]
    steps = [dest.shape[0] // tm for _, dest in streams]
    in_specs, args, first = [], [], 0
    for (h_rt, dest), nst in zip(streams, steps):
        idx = lambda i, *_, first=first, nst=nst: (jnp.clip(i - first, 0, nst - 1), 0)
        idx3 = lambda i, *_, idx=idx: idx(i) + (0,)
        in_specs += [pl.BlockSpec((1, 1, tm * TOP_K), idx3, memory_space=pltpu.SMEM),
                     pl.BlockSpec((tm * seg, LANES), idx)]
        args += [dest.reshape(nst, 1, tm * TOP_K), h_rt]
        first += nst
    grid_spec = pltpu.PrefetchScalarGridSpec(
        num_scalar_prefetch=4,
        grid=(first,),
        in_specs=in_specs,
        out_specs=pl.BlockSpec(memory_space=pltpu.HBM),
        scratch_shapes=[pltpu.VMEM((MOE_BLOCK * seg, LANES), F32),
                        pltpu.SemaphoreType.DMA(()), pltpu.SemaphoreType.DMA(())],
    )
    return pl.pallas_call(
        functools.partial(_dispatch_kernel, tm=tm, steps=tuple(steps), n_blocks=n_blocks),
        grid_spec=grid_spec,
        out_shape=jax.ShapeDtypeStruct((n_blocks * MOE_BLOCK * seg, LANES), F32),
        compiler_params=_cparams("arbitrary"),
        name="dispatch",
    )(counts, pstart, padded, n_used, *args)


def _expert_kernel(be_ref, nu_ref, xs_ref, wg_ref, bg_ref, wu_ref, bu_ref, wd_ref, bd_ref, y_ref, x2d):
    i = pl.program_id(0)
    rows = MOE_BLOCK
    nseg = x2d.shape[1] // LANES

    @pl.when(i < nu_ref[0])
    def _():
        for s in range(nseg):
            x2d[:, s * LANES:(s + 1) * LANES] = xs_ref[pl.ds(s, rows, stride=SUBLANES), :].astype(BF16)
        x = x2d[...]
        a = jnp.minimum(jnp.dot(x, wg_ref[...], preferred_element_type=F32) + bg_ref[...], SWIGLU_LIMIT)
        u = jnp.clip(jnp.dot(x, wu_ref[...], preferred_element_type=F32) + bu_ref[...], -SWIGLU_LIMIT, SWIGLU_LIMIT)
        act = (a * _sigmoid(SWIGLU_ALPHA * a) * (u + 1.0)).astype(BF16)
        y = jnp.dot(act, wd_ref[...], preferred_element_type=F32) + bd_ref[...]
        for s in range(nseg):
            y_ref[pl.ds(s, rows, stride=SUBLANES), :] = y[:, s * LANES:(s + 1) * LANES]

    @pl.when(i >= nu_ref[0])
    def _():
        y_ref[...] = jnp.zeros(y_ref.shape, F32)


def _experts(xs, block_e, n_used, w_gate, b_gate, w_up, b_up, w_down, b_down):
    n_blocks = block_e.shape[0]
    ne, d, ff = w_gate.shape
    seg = d // LANES
    blk = lambda i, nu: jnp.minimum(i, jnp.maximum(nu[0] - 1, 0))
    wspec = lambda shp: pl.BlockSpec((None,) + shp, lambda i, be, nu: (be[blk(i, nu)], 0, 0))
    grid_spec = pltpu.PrefetchScalarGridSpec(
        num_scalar_prefetch=2,
        grid=(n_blocks,),
        in_specs=[pl.BlockSpec((MOE_BLOCK * seg, LANES), lambda i, be, nu: (blk(i, nu), 0)),
                  wspec((d, ff)), wspec((1, ff)), wspec((d, ff)), wspec((1, ff)), wspec((ff, d)), wspec((1, d))],
        out_specs=pl.BlockSpec((MOE_BLOCK * seg, LANES), lambda i, be, nu: (i, 0)),
        scratch_shapes=[pltpu.VMEM((MOE_BLOCK, d), BF16)],
    )
    return pl.pallas_call(
        _expert_kernel,
        grid_spec=grid_spec,
        out_shape=jax.ShapeDtypeStruct((n_blocks * MOE_BLOCK * seg, LANES), F32),
        compiler_params=_cparams("arbitrary"),
        name="experts",
    )(block_e, n_used, xs, w_gate, b_gate.reshape(ne, 1, ff), w_up, b_up.reshape(ne, 1, ff),
      w_down, b_down.reshape(ne, 1, d))


def _combine_kernel(dest_ref, y_hbm, x_ref, tg_ref, gt_ref, o_ref, buf, sem, *, tm):
    n = tm * TOP_K

    def issue(blk, carry):
        for tt in range(TOK_UNROLL):
            t = blk * TOK_UNROLL + tt
            for j in range(TOP_K):
                _row_copy(y_hbm, dest_ref[0, 0, t * TOP_K + j], buf, j * tm + t, sem).start()
        return carry

    lax.fori_loop(0, tm // TOK_UNROLL, issue, 0)
    pltpu.make_async_copy(y_hbm.at[pl.ds(0, n * SUBLANES), :], buf, sem).wait()
    gt = gt_ref[...]
    tg = tg_ref[...]
    for s in range(o_ref.shape[1] // LANES):
        acc = tg[:, 0:1] * buf[pl.ds(s, tm, stride=SUBLANES), :]
        for j in range(1, TOP_K):
            acc = acc + tg[:, j:j + 1] * buf[pl.ds(j * tm * SUBLANES + s, tm, stride=SUBLANES), :]
        sl = slice(s * LANES, (s + 1) * LANES)
        o_ref[:, sl] = x_ref[:, sl] + gt[:, sl] * acc


def _combine(x, y_rt, dest, top_g, gt2, rows_per_group, tm=128):
    t, d = x.shape
    ng = gt2.shape[0]
    tpg = rows_per_group // tm
    return pl.pallas_call(
        functools.partial(_combine_kernel, tm=tm),
        grid=(t // tm,),
        in_specs=[pl.BlockSpec((1, 1, tm * TOP_K), lambda i: (i, 0, 0), memory_space=pltpu.SMEM),
                  pl.BlockSpec(memory_space=pltpu.HBM),
                  pl.BlockSpec((tm, d), lambda i: (i, 0)),
                  pl.BlockSpec((tm, TOP_K), lambda i: (i, 0)),
                  pl.BlockSpec((None, 1, d), lambda i: (jnp.minimum(i // tpg, ng - 1), 0, 0))],
        out_specs=pl.BlockSpec((tm, d), lambda i: (i, 0)),
        out_shape=jax.ShapeDtypeStruct((t, d), F32),
        scratch_shapes=[pltpu.VMEM((tm * TOP_K * SUBLANES, LANES), F32), pltpu.SemaphoreType.DMA(())],
        compiler_params=_cparams("arbitrary"),
        name="combine",
    )(dest.reshape(t // tm, 1, tm * TOP_K), y_rt, x, top_g, gt2)


def _final_norm_kernel(x_ref, g_ref, o_ref):
    x = x_ref[...]
    o_ref[...] = x * lax.rsqrt(jnp.mean(x * x, axis=-1, keepdims=True) + EPS) * g_ref[...]


def _final_norm(x, g, tm=1024):
    t, d = x.shape
    return pl.pallas_call(
        _final_norm_kernel,
        grid=(t // tm,),
        in_specs=[pl.BlockSpec((tm, d), lambda i: (i, 0)), pl.BlockSpec((1, d), lambda i: (0, 0))],
        out_specs=pl.BlockSpec((tm, d), lambda i: (i, 0)),
        out_shape=jax.ShapeDtypeStruct((t, d), F32),
        compiler_params=_cparams("parallel"),
        name="final_norm",
    )(x, g.reshape(1, d))


def _slot_tables(cnt_row, n_tok):
    n_blocks = -(-n_tok * TOP_K // MOE_BLOCK) + N_EXPERTS
    counts = cnt_row[0, :N_EXPERTS].astype(jnp.int32)
    padded = (counts + MOE_BLOCK - 1) // MOE_BLOCK * MOE_BLOCK
    padded_end = jnp.cumsum(padded)
    pstart = padded_end - padded
    blk_start = jnp.arange(n_blocks, dtype=jnp.int32) * MOE_BLOCK
    block_e = jnp.minimum(jnp.sum(padded_end[None, :] <= blk_start[:, None], axis=1), N_EXPERTS - 1).astype(jnp.int32)
    n_used = (padded_end[-1:] // MOE_BLOCK).astype(jnp.int32)
    return counts, pstart.astype(jnp.int32), padded.astype(jnp.int32), block_e, n_used, n_blocks


def _slot_of(top_i, rank, pstart):
    e = jnp.arange(N_EXPERTS, dtype=jnp.int32)
    return rank + jnp.sum(jnp.where(top_i[..., None] == e, pstart, 0), axis=-1)


def _gate_tables(gb, bsz, seq_len):
    g5 = gb.reshape(bsz, seq_len, 2, 2, DN_HEADS)
    gbh = g5.transpose(0, 4, 1, 2, 3).reshape(bsz, DN_HEADS, seq_len, 4)
    grow = g5[:, :, :, 0, :].transpose(0, 3, 2, 1).reshape(bsz, DN_HEADS, 2, seq_len // GROUP, GROUP)
    return gbh, grow


def kernel(x, c, ctx, c_ctx, w_mod, b_mod, g_norm1, w_in, conv_w, a_log, dt_bias, g_out_norm, w_out, g_norm2,
           w_router, b_router, w_gate, b_gate, w_up, b_up, w_down, b_down, g_final):
    bsz, seq, d = x.shape
    clen = ctx.shape[1]
    depth = w_mod.shape[0]
    fw = FOURIER_GROUPS * LANES
    nmain = fw + 4 * DN_HEADS * DN_HEAD_DIM
    nab = 4 * DN_HEADS

    xt = x.reshape(bsz * seq, d)
    ct = ctx.reshape(bsz * clen, d)
    c_rows = jnp.concatenate([c, c_ctx[None, :], jnp.zeros((2 * SUBLANES - bsz - 1, d), F32)], axis=0)

    cp_x, sp_x = _dft_tables(seq)
    cp_c, sp_c = _dft_tables(clen)
    ch = np.arange(LANES, dtype=np.int64)
    ang = 2.0 * np.pi * ((ch[:, None] * ch[None, :]) % LANES) / LANES

    def chan(tab, n):
        return jnp.asarray(tab / math.sqrt(n * LANES), F32).astype(BF16)

    lanepad = lambda v: jnp.pad(v, ((0, 0), (0, LANES - v.shape[1])))
    zeros4 = jnp.zeros((DN_HEADS,), F32)
    isdec_row = lanepad(jnp.tile(jnp.concatenate([jnp.ones((DN_HEADS,), F32), zeros4]), 2)[None, :])

    for l in range(depth):
        update_ctx = l < depth - 1
        mod = _mod(c_rows, w_mod[l], b_mod[l]).reshape(2 * SUBLANES, 6, 1, d)
        mx = lambda j: mod[:bsz, j]
        mc = lambda j: mod[bsz:bsz + 1, j]

        w_main = w_in[l][:, :nmain].astype(BF16)
        w_ab = lanepad(w_in[l][:, nmain:]).astype(BF16)
        alog_row = lanepad(jnp.concatenate([a_log[l][0], zeros4, a_log[l][1], zeros4])[None, :])
        dtb_row = lanepad(jnp.concatenate([dt_bias[l][0], zeros4, dt_bias[l][1], zeros4])[None, :])
        ip = lambda tok, sh, sc, rpg: _inproj(tok, g_norm1[l], sc, sh, w_main, w_ab, alog_row, dtb_row,
                                               isdec_row, rpg)
        f_x, qkv_x, z_x, gb_x = ip(xt, mx(0), mx(1), seq)
        f_c, qkv_c, z_c, gb_c = ip(ct, mc(0), mc(1), bsz * clen)

        gbh_c, grow_c = _gate_tables(gb_c, bsz, clen)
        gbh_x, grow_x = _gate_tables(gb_x, bsz, seq)
        s_zero = jnp.zeros((bsz, DN_HEADS, DN_HEAD_DIM, DN_HEAD_DIM), F32)
        ocf, ocb, s_f, s_b = _deltanet(qkv_c, conv_w[l], gbh_c, grow_c, s_zero, s_zero, bsz, clen)
        oxf, oxb, _, _ = _deltanet(qkv_x, conv_w[l], gbh_x, grow_x, s_f, s_b, bsz, seq)

        w_out_b = w_out[l].astype(BF16)
        w_r = lanepad(w_router[l])
        b_r = lanepad(b_router[l][None, :])
        four_x = _fourier(f_x, cp_x, sp_x, chan(np.cos(ang), seq), chan(np.sin(ang), seq), bsz, seq)
        cnt0 = jnp.zeros((1, LANES), F32)
        xn_x, h2_x, ti_x, tg_x, rk_x, cnt = _outproj(four_x, oxf, oxb, z_x, xt, g_out_norm[l], mx(2), mx(4), mx(3),
                                                      g_norm2[l], w_out_b, w_r, b_r, cnt0, seq)
        n_tok = bsz * seq
        if update_ctx:
            four_c = _fourier(f_c, cp_c, sp_c, chan(np.cos(ang), clen), chan(np.sin(ang), clen), bsz, clen)
            xn_c, h2_c, ti_c, tg_c, rk_c, cnt = _outproj(four_c, ocf, ocb, z_c, ct, g_out_norm[l], mc(2), mc(4),
                                                          mc(3), g_norm2[l], w_out_b, w_r, b_r, cnt, bsz * clen)
            n_tok += bsz * clen

        counts, pstart, padded, block_e, n_used, n_blocks = _slot_tables(cnt, n_tok)
        dest_x = _slot_of(ti_x, rk_x, pstart)
        streams = [(h2_x, dest_x)]
        if update_ctx:
            dest_c = _slot_of(ti_c, rk_c, pstart)
            streams.append((h2_c, dest_c))
        xs = _dispatch(streams, counts, pstart, padded, n_used, n_blocks)
        y_rt = _experts(xs, block_e, n_used, w_gate[l].astype(BF16), b_gate[l], w_up[l].astype(BF16), b_up[l],
                        w_down[l].astype(BF16), b_down[l])
        xt = _combine(xn_x, y_rt, dest_x, tg_x, mx(5), seq)
        if update_ctx:
            ct = _combine(xn_c, y_rt, dest_c, tg_c, mc(5), bsz * clen)

    return _final_norm(xt, g_final).reshape(bsz, seq, d)
```

```python
import functools
import math

import numpy as np
import jax
import jax.numpy as jnp
from jax import lax
from jax.experimental import pallas as pl
from jax.experimental.pallas import tpu as pltpu

FOURIER_GROUPS = 4
DN_HEADS = 4
DN_HEAD_DIM = 128
CONV_K = 5
N_EXPERTS = 32
TOP_K = 4
SWIGLU_LIMIT = 7.0
SWIGLU_ALPHA = 1.702
EPS = 1e-6

LANES = 128
SUBLANES = 8
V7X_VMEM_BYTES = 64 * 1024 * 1024
VMEM_LIMIT = V7X_VMEM_BYTES - 8 * 1024 * 1024
CHUNK = 64
GROUP = 4 * CHUNK
MOE_BLOCK = 256

F32 = jnp.float32
BF16 = jnp.bfloat16


def _cparams(*sem):
    return pltpu.CompilerParams(dimension_semantics=sem, vmem_limit_bytes=VMEM_LIMIT)


def _sigmoid(x):
    return 1.0 / (1.0 + jnp.exp(-x))


def _mod_kernel(c_ref, w_ref, b_ref, o_ref):
    c = c_ref[...]
    a = c * _sigmoid(c)
    o_ref[...] = jnp.dot(a, w_ref[...], preferred_element_type=F32) + b_ref[...]


def _mod(c_rows, w_mod, b_mod):
    r, d = c_rows.shape
    n = w_mod.shape[1]
    tn = n // 4
    return pl.pallas_call(
        _mod_kernel,
        grid=(n // tn,),
        in_specs=[pl.BlockSpec((r, d), lambda j: (0, 0)),
                  pl.BlockSpec((d, tn), lambda j: (0, j)),
                  pl.BlockSpec((1, tn), lambda j: (0, j))],
        out_specs=pl.BlockSpec((r, tn), lambda j: (0, j)),
        out_shape=jax.ShapeDtypeStruct((r, n), F32),
        compiler_params=_cparams("parallel"),
        name="mod",
    )(c_rows, w_mod, b_mod.reshape(1, n))


def _inproj_kernel(x_ref, g_ref, sc_ref, sh_ref, w_ref, wab_ref, alog_ref, dtb_ref, isdec_ref,
                   f_ref, qkv_ref, z_ref, gb_ref, *, fw, qw):
    x = x_ref[...]
    inv = lax.rsqrt(jnp.mean(x * x, axis=-1, keepdims=True) + EPS)
    h = (x * inv * g_ref[...]) * (1.0 + sc_ref[...]) + sh_ref[...]
    hb = h.astype(BF16)
    f_ref[...] = jnp.dot(hb, w_ref[:, :fw], preferred_element_type=F32)
    for j in range(3):
        lo = fw + j * qw
        qkv_ref[:, j * qw:(j + 1) * qw] = jnp.dot(hb, w_ref[:, lo:lo + qw], preferred_element_type=F32)
    lo = fw + 3 * qw
    z_ref[...] = jnp.dot(hb, w_ref[:, lo:lo + qw], preferred_element_type=F32)
    ab = jnp.dot(hb, wab_ref[...], preferred_element_type=F32)
    pre = ab + dtb_ref[...]
    softplus = jnp.maximum(pre, 0.0) + jnp.log(1.0 + jnp.exp(-jnp.abs(pre)))
    dec = -jnp.exp(alog_ref[...]) * softplus
    gb = jnp.where(isdec_ref[...] > 0.0, dec, _sigmoid(ab))
    gb_ref[...] = gb[:, :gb_ref.shape[1]]


def _inproj(x, g1, sc, sh, w_main, w_ab, alog_row, dtb_row, isdec_row, rows_per_group, tm=512):
    t, d = x.shape
    ng = sc.shape[0]
    fw = FOURIER_GROUPS * LANES
    qw = DN_HEADS * DN_HEAD_DIM
    nab = 4 * DN_HEADS
    tpg = rows_per_group // tm
    grp = lambda i: (jnp.minimum(i // tpg, ng - 1), 0, 0)
    const = lambda i: (0, 0)
    return pl.pallas_call(
        functools.partial(_inproj_kernel, fw=fw, qw=qw),
        grid=(t // tm,),
        in_specs=[pl.BlockSpec((tm, d), lambda i: (i, 0)),
                  pl.BlockSpec((1, d), const),
                  pl.BlockSpec((None, 1, d), grp),
                  pl.BlockSpec((None, 1, d), grp),
                  pl.BlockSpec(w_main.shape, const),
                  pl.BlockSpec(w_ab.shape, const),
                  pl.BlockSpec((1, LANES), const),
                  pl.BlockSpec((1, LANES), const),
                  pl.BlockSpec((1, LANES), const)],
        out_specs=[pl.BlockSpec((tm, fw), lambda i: (i, 0)),
                   pl.BlockSpec((tm, 3 * qw), lambda i: (i, 0)),
                   pl.BlockSpec((tm, qw), lambda i: (i, 0)),
                   pl.BlockSpec((tm, nab), lambda i: (i, 0))],
        out_shape=[jax.ShapeDtypeStruct((t, fw), F32),
                   jax.ShapeDtypeStruct((t, 3 * qw), F32),
                   jax.ShapeDtypeStruct((t, qw), F32),
                   jax.ShapeDtypeStruct((t, nab), F32)],
        compiler_params=_cparams("parallel"),
        name="inproj",
    )(x, g1.reshape(1, d), sc, sh, w_main, w_ab, alog_row, dtb_row, isdec_row)


def _conv_act(src_ref, w_ref, dst_ref, seq_len, l2norm, scale):
    r = GROUP
    nt = seq_len // r
    halo = SUBLANES
    w = w_ref[...]

    def body(i, carry):
        r0 = pl.multiple_of(i * r, r)
        cur = src_ref[pl.ds(r0, r), :]
        p0 = pl.multiple_of(jnp.maximum(r0 - halo, 0), halo)
        n0 = pl.multiple_of(jnp.minimum(r0 + r, seq_len - halo), halo)
        prev = jnp.where(i > 0, src_ref[pl.ds(p0, halo), :], 0.0)
        nxt = jnp.where(i < nt - 1, src_ref[pl.ds(n0, halo), :], 0.0)
        win = jnp.concatenate([prev, cur, nxt], axis=0)
        n = r + 2 * halo
        acc = None
        for j in range(CONV_K):
            off = halo - CONV_K // 2 + j
            sh = pltpu.roll(win, n - off, 0)[:r]
            term = w[j:j + 1, :] * sh
            acc = term if acc is None else acc + term
        y = acc * _sigmoid(acc)
        if l2norm:
            segs = []
            for hh in range(y.shape[1] // DN_HEAD_DIM):
                ys = y[:, hh * DN_HEAD_DIM:(hh + 1) * DN_HEAD_DIM]
                segs.append(ys * (lax.rsqrt(jnp.sum(ys * ys, axis=-1, keepdims=True) + EPS) * scale))
            y = jnp.concatenate(segs, axis=1)
        dst_ref[pl.ds(r0, r), :] = y
        return carry

    lax.fori_loop(0, nt, body, 0)


def _dot_nt(a, b):
    return lax.dot_general(a, b, (((1,), (1,)), ((), ())), preferred_element_type=F32)


def _dot_tn(a, b):
    return lax.dot_general(a, b, (((0,), (0,)), ((), ())), preferred_element_type=F32)


def _dn_masks(dirn):
    ii = lax.broadcasted_iota(jnp.int32, (GROUP, GROUP), 0)
    jj = lax.broadcasted_iota(jnp.int32, (GROUP, GROUP), 1)
    same = (ii // CHUNK) == (jj // CHUNK)
    if dirn == 0:
        return same, same & (ii >= jj), same & (ii > jj), same & (ii <= jj)
    return same, same & (ii <= jj), same & (ii < jj), same & (ii >= jj)


def _dn_groups(chains, states, q_s, k_s, v_s, gb_ref, gr_ref, o_refs):
    rng = range(len(chains))
    dh = DN_HEAD_DIM
    masks = {d: _dn_masks(d) for d in sorted({c[1] for c in chains})}
    same = [masks[c[1]][0] for c in chains]

    def pack(bd):
        return bd[0:CHUNK] + bd[CHUNK:2 * CHUNK] + bd[2 * CHUNK:3 * CHUNK] + bd[3 * CHUNK:]

    def unpack(pk, sm):
        return jnp.where(sm, jnp.concatenate([pk, pk, pk, pk], axis=0), 0.0)

    r0s, q, k, v, bcol, gam_col, tot_col, dec = [], [], [], [], [], [], [], []
    for hh, dirn, g in chains:
        r0 = pl.multiple_of(g * GROUP, GROUP)
        cols = slice(hh * dh, (hh + 1) * dh)
        r0s.append(r0)
        q.append(q_s[pl.ds(r0, GROUP), cols])
        k.append(k_s[pl.ds(r0, GROUP), cols])
        v.append(v_s[pl.ds(r0, GROUP), cols])
        gcol = gb_ref[hh, pl.ds(r0, GROUP), 2 * dirn:2 * dirn + 1]
        bcol.append(gb_ref[hh, pl.ds(r0, GROUP), 2 * dirn + 1:2 * dirn + 2])
        grow = gr_ref[hh, dirn, pl.ds(g, 1), :]
        sm, incl, _, incl_t = masks[dirn]
        gc = jnp.sum(jnp.where(incl, grow, 0.0), axis=1, keepdims=True)
        gr = jnp.sum(jnp.where(incl_t, gcol, 0.0), axis=0, keepdims=True)
        gam_col.append(gc)
        tot_col.append(jnp.sum(jnp.where(sm, grow, 0.0), axis=1, keepdims=True))
        dec.append(jnp.where(incl, jnp.exp(jnp.where(incl, gc - gr, 0.0)), 0.0))

    kk = [_dot_nt(k[i], k[i]) for i in rng]
    qk = [_dot_nt(q[i], k[i]) for i in rng]
    mk_bd = [jnp.where(masks[chains[i][1]][2], -(bcol[i] * kk[i] * dec[i]), 0.0) for i in rng]
    attn = [qk[i] * dec[i] for i in rng]

    pi = lax.broadcasted_iota(jnp.int32, (CHUNK, GROUP), 0)
    pj = lax.broadcasted_iota(jnp.int32, (CHUNK, GROUP), 1)
    eye_pk = jnp.where(pi == (pj % CHUNK), 1.0, 0.0)
    mk_pk = [pack(mk_bd[i]) for i in rng]
    p = [eye_pk + mk_pk[i] for i in rng]
    for _ in range(5):
        mk_pk = [jnp.dot(mk_pk[i], mk_bd[i], preferred_element_type=F32) for i in rng]
        mk_bd = [unpack(mk_pk[i], same[i]) for i in rng]
        p = [p[i] + jnp.dot(p[i], mk_bd[i], preferred_element_type=F32) for i in rng]
    t_bd = [unpack(p[i], same[i]) for i in rng]

    eg = [jnp.exp(gam_col[i]) for i in rng]
    rhs = [jnp.concatenate([v[i] * bcol[i], k[i] * (bcol[i] * eg[i])], axis=1) for i in rng]
    uw = [jnp.dot(t_bd[i], rhs[i], preferred_element_type=F32) for i in rng]
    qg = [q[i] * eg[i] for i in rng]
    kend = [k[i] * jnp.exp(tot_col[i] - gam_col[i]) for i in rng]
    gend = [jnp.exp(tot_col[i]) for i in rng]

    zeros = jnp.zeros((CHUNK, dh), F32)
    for step in range(GROUP // CHUNK):
        cs = [step if chains[i][1] == 0 else GROUP // CHUNK - 1 - step for i in rng]
        lo = [c * CHUNK for c in cs]
        lhs = [jnp.concatenate([uw[i][lo[i]:lo[i] + CHUNK, dh:], qg[i][lo[i]:lo[i] + CHUNK]], axis=0)
               for i in rng]
        wq = [jnp.dot(lhs[i], states[i], preferred_element_type=F32) for i in rng]
        v_new = [uw[i][lo[i]:lo[i] + CHUNK, :dh] - wq[i][:CHUNK] for i in rng]
        states = [states[i] * gend[i][lo[i]:lo[i] + 1] + _dot_tn(kend[i][lo[i]:lo[i] + CHUNK], v_new[i])
                  for i in rng]
        for i in rng:
            v_full = jnp.concatenate([v_new[i] if cc == cs[i] else zeros for cc in range(GROUP // CHUNK)], axis=0)
            o_c = wq[i][CHUNK:] + jnp.dot(attn[i][lo[i]:lo[i] + CHUNK], v_full, preferred_element_type=F32)
            hh = chains[i][0]
            o_refs[chains[i][1]][pl.ds(r0s[i] + lo[i], CHUNK), hh * dh:(hh + 1) * dh] = o_c
    return states


def _dn_kernel(qp_ref, kp_ref, vp_ref, cwq_ref, cwk_ref, cwv_ref, gb_ref, gr_ref, s0f_ref, s0b_ref,
               of_ref, ob_ref, sf_ref, sb_ref, q_s, k_s, v_s, *, seq_len, nh):
    ng = seq_len // GROUP
    _conv_act(qp_ref, cwq_ref, q_s, seq_len, True, DN_HEAD_DIM ** -0.5)
    _conv_act(kp_ref, cwk_ref, k_s, seq_len, True, 1.0)
    _conv_act(vp_ref, cwv_ref, v_s, seq_len, False, 1.0)

    def body(m, carry):
        chains = [(hh, d, m if d == 0 else ng - 1 - m) for hh in range(nh) for d in (0, 1)]
        return tuple(_dn_groups(chains, list(carry), q_s, k_s, v_s, gb_ref, gr_ref, (of_ref, ob_ref)))

    init = tuple(r[hh] for hh in range(nh) for r in (s0f_ref, s0b_ref))
    fin = lax.fori_loop(0, ng, body, init)
    for hh in range(nh):
        sf_ref[hh] = fin[2 * hh]
        sb_ref[hh] = fin[2 * hh + 1]


def _deltanet(qkv, conv_w, gbh, grow, s0f, s0b, bsz, seq_len, nh=2):
    h, dh = DN_HEADS, DN_HEAD_DIM
    ng = seq_len // GROUP
    nhb = h // nh
    w = nh * dh
    once = pl.Buffered(1)
    col = lambda off: (lambda b, hb: (b, off + hb))
    wcol = lambda off: (lambda b, hb: (0, off + hb))
    st = pl.BlockSpec((None, nh, dh, dh), lambda b, hb: (b, hb, 0, 0))
    return pl.pallas_call(
        functools.partial(_dn_kernel, seq_len=seq_len, nh=nh),
        grid=(bsz, nhb),
        in_specs=[pl.BlockSpec((seq_len, w), col(0), pipeline_mode=once),
                  pl.BlockSpec((seq_len, w), col(nhb), pipeline_mode=once),
                  pl.BlockSpec((seq_len, w), col(2 * nhb), pipeline_mode=once),
                  pl.BlockSpec((CONV_K, w), wcol(0)),
                  pl.BlockSpec((CONV_K, w), wcol(nhb)),
                  pl.BlockSpec((CONV_K, w), wcol(2 * nhb)),
                  pl.BlockSpec((None, nh, seq_len, 4), lambda b, hb: (b, hb, 0, 0), pipeline_mode=once),
                  pl.BlockSpec((None, nh, 2, ng, GROUP), lambda b, hb: (b, hb, 0, 0, 0)),
                  st, st],
        out_specs=[pl.BlockSpec((seq_len, w), col(0)),
                   pl.BlockSpec((seq_len, w), col(0)),
                   st, st],
        out_shape=[jax.ShapeDtypeStruct((bsz * seq_len, h * dh), F32),
                   jax.ShapeDtypeStruct((bsz * seq_len, h * dh), F32),
                   jax.ShapeDtypeStruct((bsz, h, dh, dh), F32),
                   jax.ShapeDtypeStruct((bsz, h, dh, dh), F32)],
        scratch_shapes=[pltpu.VMEM((seq_len, w), F32)] * 3,
        compiler_params=_cparams("parallel", "parallel"),
        name="deltanet",
    )(qkv, qkv, qkv, conv_w, conv_w, conv_w, gbh, grow, s0f, s0b)


def _dft_table_kernel(tac_ref, tas_ref, tbc_ref, tbs_ref, cp_ref, sp_ref, *, nt1):
    tbc = tbc_ref[...]
    tbs = tbs_ref[...]
    for t1 in range(nt1):
        ac = tac_ref[:, t1:t1 + 1]
        asn = tas_ref[:, t1:t1 + 1]
        cp_ref[:, t1 * LANES:(t1 + 1) * LANES] = (ac * tbc - asn * tbs).astype(BF16)
        sp_ref[:, t1 * LANES:(t1 + 1) * LANES] = (asn * tbc + ac * tbs).astype(BF16)


def _dft_tables(n):
    nt1 = n // LANES
    s = np.arange(n, dtype=np.int64)[:, None]
    t1 = np.arange(nt1, dtype=np.int64)[None, :]
    t0 = np.arange(LANES, dtype=np.int64)[None, :]
    ang_a = 2.0 * np.pi * ((s * t1) % nt1) / nt1
    ang_b = 2.0 * np.pi * ((s * t0) % n) / n
    pad = ((0, 0), (0, LANES - nt1))
    tac = jnp.asarray(np.pad(np.cos(ang_a), pad), F32)
    tas = jnp.asarray(np.pad(np.sin(ang_a), pad), F32)
    tbc = jnp.asarray(np.cos(ang_b), F32)
    tbs = jnp.asarray(np.sin(ang_b), F32)
    tr = min(n, 256)
    small = pl.BlockSpec((tr, LANES), lambda i: (i, 0))
    big = pl.BlockSpec((tr, n), lambda i: (i, 0))
    return pl.pallas_call(
        functools.partial(_dft_table_kernel, nt1=nt1),
        grid=(n // tr,),
        in_specs=[small, small, small, small],
        out_specs=[big, big],
        out_shape=[jax.ShapeDtypeStruct((n, n), BF16)] * 2,
        compiler_params=_cparams("parallel"),
        name="dft_tables",
    )(tac, tas, tbc, tbs)


def _fourier_kernel(f_ref, cp_ref, sp_ref, cc_ref, sc_ref, o_ref):
    kstep = pl.program_id(1)
    xb = f_ref[...].astype(BF16)
    cc = cc_ref[...]
    sc = sc_ref[...]
    zc = [jnp.dot(xb[:, g * LANES:(g + 1) * LANES], cc, preferred_element_type=F32) for g in range(FOURIER_GROUPS)]
    zs = [jnp.dot(xb[:, g * LANES:(g + 1) * LANES], sc, preferred_element_type=F32) for g in range(FOURIER_GROUPS)]
    zc = jnp.concatenate(zc, axis=1).astype(BF16)
    zs = jnp.concatenate(zs, axis=1).astype(BF16)
    part = (jnp.dot(cp_ref[...], zc, preferred_element_type=F32)
            - jnp.dot(sp_ref[...], zs, preferred_element_type=F32))

    @pl.when(kstep == 0)
    def _():
        o_ref[...] = part

    @pl.when(kstep > 0)
    def _():
        o_ref[...] += part


def _fourier(f, cp, sp, cc, sc, bsz, seq_len):
    fw = f.shape[1]
    tk = min(seq_len, 512)
    return pl.pallas_call(
        _fourier_kernel,
        grid=(bsz, seq_len // tk),
        in_specs=[pl.BlockSpec((tk, fw), lambda b, k: (b * (seq_len // tk) + k, 0)),
                  pl.BlockSpec((seq_len, tk), lambda b, k: (0, k)),
                  pl.BlockSpec((seq_len, tk), lambda b, k: (0, k)),
                  pl.BlockSpec((LANES, LANES), lambda b, k: (0, 0)),
                  pl.BlockSpec((LANES, LANES), lambda b, k: (0, 0))],
        out_specs=pl.BlockSpec((seq_len, fw), lambda b, k: (b, 0)),
        out_shape=jax.ShapeDtypeStruct((bsz * seq_len, fw), F32),
        compiler_params=_cparams("parallel", "arbitrary"),
        name="fourier",
    )(f, cp, sp, cc, sc)


def _outproj_kernel(four_ref, of_ref, ob_ref, z_ref, x_ref, gout_ref, gt1_ref, sc2_ref, sh2_ref, g2_ref,
                    wout_ref, wr_ref, br_ref, cnt0_ref, xn_ref, h2_ref, ti_ref, tg_ref, rk_ref, cnt_ref,
                    carry, *, fw):
    @pl.when(pl.program_id(0) == 0)
    def _():
        carry[...] = cnt0_ref[...]

    o = of_ref[...] + ob_ref[...]
    z = z_ref[...]
    gout = gout_ref[...]
    parts = []
    for hh in range(DN_HEADS):
        sl = slice(hh * DN_HEAD_DIM, (hh + 1) * DN_HEAD_DIM)
        oh = o[:, sl]
        zh = z[:, sl]
        oh = oh * lax.rsqrt(jnp.mean(oh * oh, axis=-1, keepdims=True) + EPS) * gout
        parts.append(oh * (zh * _sigmoid(zh)))
    gated = jnp.concatenate(parts, axis=1).astype(BF16)
    mix = (jnp.dot(four_ref[...].astype(BF16), wout_ref[:fw, :], preferred_element_type=F32)
           + jnp.dot(gated, wout_ref[fw:, :], preferred_element_type=F32))
    xn = x_ref[...] + gt1_ref[...] * mix
    xn_ref[...] = xn
    inv = lax.rsqrt(jnp.mean(xn * xn, axis=-1, keepdims=True) + EPS)
    h2 = (xn * inv * g2_ref[...]) * (1.0 + sc2_ref[...]) + sh2_ref[...]
    tm = h2.shape[0]
    for s in range(h2.shape[1] // LANES):
        h2_ref[pl.ds(s, tm, stride=SUBLANES), :] = h2[:, s * LANES:(s + 1) * LANES]
    logits = jnp.dot(h2, wr_ref[...], preferred_element_type=F32, precision=lax.Precision.HIGHEST) + br_ref[...]
    lane = lax.broadcasted_iota(jnp.int32, logits.shape, 1)
    neg = jnp.float32(-jnp.inf)
    logits = jnp.where(lane < N_EXPERTS, logits, neg)
    vals, idxs = [], []
    for _ in range(TOP_K):
        m = jnp.max(logits, axis=-1, keepdims=True)
        idx = jnp.min(jnp.where(logits == m, lane, LANES), axis=-1, keepdims=True)
        vals.append(m)
        idxs.append(idx)
        logits = jnp.where(lane == idx, neg, logits)
    ex = [jnp.exp(vv - vals[0]) for vv in vals]
    den = ex[0] + ex[1] + ex[2] + ex[3]
    onehot = jnp.zeros(logits.shape, F32)
    for j in range(TOP_K):
        onehot = onehot + jnp.where(lane == idxs[j], 1.0, 0.0)
    ri = lax.broadcasted_iota(jnp.int32, (tm, tm), 0)
    ci = lax.broadcasted_iota(jnp.int32, (tm, tm), 1)
    tri = jnp.where(ri > ci, 1.0, 0.0).astype(BF16)
    base = jnp.dot(tri, onehot.astype(BF16), preferred_element_type=F32) + carry[...]
    carry[...] = carry[...] + jnp.sum(onehot, axis=0, keepdims=True)
    cnt_ref[...] = carry[...]
    kl = lax.broadcasted_iota(jnp.int32, (tm, TOP_K), 1)
    ti = jnp.zeros((tm, TOP_K), jnp.int32)
    tg = jnp.zeros((tm, TOP_K), F32)
    rk = jnp.zeros((tm, TOP_K), jnp.int32)
    for j in range(TOP_K):
        rank_j = jnp.sum(jnp.where(lane == idxs[j], base, 0.0), axis=-1, keepdims=True).astype(jnp.int32)
        ti = jnp.where(kl == j, idxs[j], ti)
        tg = jnp.where(kl == j, ex[j] / den, tg)
        rk = jnp.where(kl == j, rank_j, rk)
    ti_ref[...] = ti
    tg_ref[...] = tg
    rk_ref[...] = rk


def _outproj(four, o_f, o_b, z, x, g_out, gt1, sc2, sh2, g2, w_out, w_r, b_r, cnt0, rows_per_group, tm=256):
    t, d = x.shape
    fw = four.shape[1]
    ng = gt1.shape[0]
    tpg = rows_per_group // tm
    grp = lambda i: (jnp.minimum(i // tpg, ng - 1), 0, 0)
    const = lambda i: (0, 0)
    row = lambda w: pl.BlockSpec((tm, w), lambda i: (i, 0))
    gspec = pl.BlockSpec((None, 1, d), grp)
    return pl.pallas_call(
        functools.partial(_outproj_kernel, fw=fw),
        grid=(t // tm,),
        in_specs=[row(fw), row(z.shape[1]), row(z.shape[1]), row(z.shape[1]), row(d),
                  pl.BlockSpec((1, DN_HEAD_DIM), const), gspec, gspec, gspec,
                  pl.BlockSpec((1, d), const),
                  pl.BlockSpec(w_out.shape, const),
                  pl.BlockSpec(w_r.shape, const),
                  pl.BlockSpec((1, LANES), const),
                  pl.BlockSpec((1, LANES), const)],
        out_specs=[row(d),
                   pl.BlockSpec((tm * SUBLANES, LANES), lambda i: (i, 0)),
                   row(TOP_K), row(TOP_K), row(TOP_K),
                   pl.BlockSpec((1, LANES), const)],
        out_shape=[jax.ShapeDtypeStruct((t, d), F32),
                   jax.ShapeDtypeStruct((t * d // LANES, LANES), F32),
                   jax.ShapeDtypeStruct((t, TOP_K), jnp.int32),
                   jax.ShapeDtypeStruct((t, TOP_K), F32),
                   jax.ShapeDtypeStruct((t, TOP_K), jnp.int32),
                   jax.ShapeDtypeStruct((1, LANES), F32)],
        scratch_shapes=[pltpu.VMEM((1, LANES), F32)],
        compiler_params=_cparams("arbitrary"),
        name="outproj",
    )(four, o_f, o_b, z, x, g_out.reshape(1, DN_HEAD_DIM), gt1, sc2, sh2, g2.reshape(1, d), w_out, w_r, b_r, cnt0)


TOK_UNROLL = 4


def _row_copy(src, src_row, dst, dst_row, sem):
    return pltpu.make_async_copy(
        src.at[pl.ds(pl.multiple_of(src_row * SUBLANES, SUBLANES), SUBLANES), :],
        dst.at[pl.ds(pl.multiple_of(dst_row * SUBLANES, SUBLANES), SUBLANES), :], sem)


def _dispatch_kernel(cnt_ref, ps_ref, pd_ref, nu_ref, *refs, tm, steps, n_blocks):
    ns = len(steps)
    dest_refs, h_refs = refs[0:2 * ns:2], refs[1:2 * ns:2]
    xs_hbm, zero_s, sem, sem_pad = refs[2 * ns:]
    i = pl.program_id(0)
    blk_rows = MOE_BLOCK * SUBLANES

    def tail_copy(b):
        return pltpu.make_async_copy(zero_s, xs_hbm.at[pl.ds(pl.multiple_of(b * blk_rows, blk_rows), blk_rows), :],
                                     sem_pad)

    @pl.when(i == 0)
    def _():
        zero_s[...] = jnp.zeros(zero_s.shape, F32)

        def per_expert(e, tot):
            def one(r, c):
                _row_copy(zero_s, 0, xs_hbm, ps_ref[e] + r, sem_pad).start()
                return c

            lax.fori_loop(cnt_ref[e], pd_ref[e], one, 0)
            return tot + pd_ref[e] - cnt_ref[e]

        n_pad = lax.fori_loop(0, N_EXPERTS, per_expert, 0)

        def start_tail(b, c):
            tail_copy(b).start()
            return c

        lax.fori_loop(nu_ref[0], n_blocks, start_tail, 0)

        def wait_row(r, c):
            _row_copy(zero_s, 0, xs_hbm, 0, sem_pad).wait()
            return c

        lax.fori_loop(0, n_pad, wait_row, 0)

        def wait_tail(b, c):
            tail_copy(b).wait()
            return c

        lax.fori_loop(nu_ref[0], n_blocks, wait_tail, 0)

    first = 0
    for dest_ref, h_ref, nst in zip(dest_refs, h_refs, steps):
        @pl.when((i >= first) & (i < first + nst))
        def _(dest_ref=dest_ref, h_ref=h_ref):
            def issue(blk, carry):
                for tt in range(TOK_UNROLL):
                    t = blk * TOK_UNROLL + tt
                    for j in range(TOP_K):
                        _row_copy(h_ref, t, xs_hbm, dest_ref[0, 0, t * TOP_K + j], sem).start(priority=j % 2)
                return carry

            lax.fori_loop(0, tm // TOK_UNROLL, issue, 0)
            for _ in range(TOP_K):
                pltpu.make_async_copy(h_ref, xs_hbm.at[pl.ds(0, tm * SUBLANES), :], sem).wait()

        first += nst


def _dispatch(streams, counts, pstart, padded, n_used, n_blocks, tm=256):
    seg = streams[0][0].shape[0] // streams[0][1].shape[0]
    steps = [dest.shape[0] // tm for _, dest in streams]
    in_specs, args, first = [], [], 0
    for (h_rt, dest), nst in zip(streams, steps):
        idx = lambda i, *_, first=first, nst=nst: (jnp.clip(i - first, 0, nst - 1), 0)
        idx3 = lambda i, *_, idx=idx: idx(i) + (0,)
        in_specs += [pl.BlockSpec((1, 1, tm * TOP_K), idx3, memory_space=pltpu.SMEM),
                     pl.BlockSpec((tm * seg, LANES), idx)]
        args += [dest.reshape(nst, 1, tm * TOP_K), h_rt]
        first += nst
    grid_spec = pltpu.PrefetchScalarGridSpec(
        num_scalar_prefetch=4,
        grid=(first,),
        in_specs=in_specs,
        out_specs=pl.BlockSpec(memory_space=pltpu.HBM),
        scratch_shapes=[pltpu.VMEM((MOE_BLOCK * seg, LANES), F32),
                        pltpu.SemaphoreType.DMA(()), pltpu.SemaphoreType.DMA(())],
    )
    return pl.pallas_call(
        functools.partial(_dispatch_kernel, tm=tm, steps=tuple(steps), n_blocks=n_blocks),
        grid_spec=grid_spec,
        out_shape=jax.ShapeDtypeStruct((n_blocks * MOE_BLOCK * seg, LANES), F32),
        compiler_params=_cparams("arbitrary"),
        name="dispatch",
    )(counts, pstart, padded, n_used, *args)


def _expert_kernel(be_ref, nu_ref, xs_ref, wg_ref, bg_ref, wu_ref, bu_ref, wd_ref, bd_ref, y_ref,
                   x2d, wg_b, wu_b, wd_b):
    i = pl.program_id(0)
    rows = MOE_BLOCK
    nseg = x2d.shape[1] // LANES

    @pl.when(i < nu_ref[0])
    def _():
        @pl.when((i == 0) | (be_ref[i] != be_ref[jnp.maximum(i - 1, 0)]))
        def _():
            wg_b[...] = wg_ref[...].astype(BF16)
            wu_b[...] = wu_ref[...].astype(BF16)
            wd_b[...] = wd_ref[...].astype(BF16)

        for s in range(nseg):
            x2d[:, s * LANES:(s + 1) * LANES] = xs_ref[pl.ds(s, rows, stride=SUBLANES), :].astype(BF16)
        x = x2d[...]
        a = jnp.minimum(jnp.dot(x, wg_b[...], preferred_element_type=F32) + bg_ref[...], SWIGLU_LIMIT)
        u = jnp.clip(jnp.dot(x, wu_b[...], preferred_element_type=F32) + bu_ref[...], -SWIGLU_LIMIT, SWIGLU_LIMIT)
        act = (a * _sigmoid(SWIGLU_ALPHA * a) * (u + 1.0)).astype(BF16)
        y = jnp.dot(act, wd_b[...], preferred_element_type=F32) + bd_ref[...]
        for s in range(nseg):
            y_ref[pl.ds(s, rows, stride=SUBLANES), :] = y[:, s * LANES:(s + 1) * LANES]

    @pl.when(i >= nu_ref[0])
    def _():
        y_ref[...] = jnp.zeros(y_ref.shape, F32)


def _experts(xs, block_e, n_used, layer, w_gate, b_gate, w_up, b_up, w_down, b_down):
    n_blocks = block_e.shape[0]
    depth, ne, d, ff = w_gate.shape
    seg = d // LANES
    blk = lambda i, nu: jnp.minimum(i, jnp.maximum(nu[0] - 1, 0))
    wspec = lambda shp: pl.BlockSpec((None, None) + shp, lambda i, be, nu: (layer, be[blk(i, nu)], 0, 0))
    grid_spec = pltpu.PrefetchScalarGridSpec(
        num_scalar_prefetch=2,
        grid=(n_blocks,),
        in_specs=[pl.BlockSpec((MOE_BLOCK * seg, LANES), lambda i, be, nu: (blk(i, nu), 0)),
                  wspec((d, ff)), wspec((1, ff)), wspec((d, ff)), wspec((1, ff)), wspec((ff, d)), wspec((1, d))],
        out_specs=pl.BlockSpec((MOE_BLOCK * seg, LANES), lambda i, be, nu: (i, 0)),
        scratch_shapes=[pltpu.VMEM((MOE_BLOCK, d), BF16), pltpu.VMEM((d, ff), BF16), pltpu.VMEM((d, ff), BF16),
                        pltpu.VMEM((ff, d), BF16)],
    )
    return pl.pallas_call(
        _expert_kernel,
        grid_spec=grid_spec,
        out_shape=jax.ShapeDtypeStruct((n_blocks * MOE_BLOCK * seg, LANES), F32),
        compiler_params=_cparams("arbitrary"),
        name="experts",
    )(block_e, n_used, xs, w_gate, b_gate.reshape(depth, ne, 1, ff), w_up, b_up.reshape(depth, ne, 1, ff),
      w_down, b_down.reshape(depth, ne, 1, d))


def _combine_kernel(dest_ref, dnext_ref, y_hbm, x_ref, tg_ref, gt_ref, o_ref, buf0, buf1, sem0, sem1, *, tm):
    n = tm * TOP_K
    i = pl.program_id(0)
    last = pl.num_programs(0) - 1

    def gather(d_ref, buf, sem):
        def issue(blk, carry):
            for tt in range(TOK_UNROLL):
                t = blk * TOK_UNROLL + tt
                for j in range(TOP_K):
                    _row_copy(y_hbm, d_ref[0, 0, t * TOP_K + j], buf, j * tm + t, sem).start(priority=j % 2)
            return carry

        lax.fori_loop(0, tm // TOK_UNROLL, issue, 0)

    def reduce(buf, sem):
        pltpu.make_async_copy(y_hbm.at[pl.ds(0, n * SUBLANES), :], buf, sem).wait()
        gt = gt_ref[...]
        tg = tg_ref[...]
        for s in range(o_ref.shape[1] // LANES):
            acc = tg[:, 0:1] * buf[pl.ds(s, tm, stride=SUBLANES), :]
            for j in range(1, TOP_K):
                acc = acc + tg[:, j:j + 1] * buf[pl.ds(j * tm * SUBLANES + s, tm, stride=SUBLANES), :]
            sl = slice(s * LANES, (s + 1) * LANES)
            o_ref[:, sl] = x_ref[:, sl] + gt[:, sl] * acc

    @pl.when(i == 0)
    def _():
        gather(dest_ref, buf0, sem0)

    for par, (cur, nxt) in enumerate((((buf0, sem0), (buf1, sem1)), ((buf1, sem1), (buf0, sem0)))):
        @pl.when(i % 2 == par)
        def _(cur=cur, nxt=nxt):
            @pl.when(i < last)
            def _():
                gather(dnext_ref, *nxt)

            reduce(*cur)


def _combine(x, y_rt, dest, top_g, gt2, rows_per_group, tm=128):
    t, d = x.shape
    ng = gt2.shape[0]
    nt = t // tm
    tpg = rows_per_group // tm
    dest3 = dest.reshape(nt, 1, tm * TOP_K)
    buf = pltpu.VMEM((tm * TOP_K * SUBLANES, LANES), F32)
    return pl.pallas_call(
        functools.partial(_combine_kernel, tm=tm),
        grid=(nt,),
        in_specs=[pl.BlockSpec((1, 1, tm * TOP_K), lambda i: (i, 0, 0), memory_space=pltpu.SMEM),
                  pl.BlockSpec((1, 1, tm * TOP_K), lambda i: (jnp.minimum(i + 1, nt - 1), 0, 0),
                               memory_space=pltpu.SMEM),
                  pl.BlockSpec(memory_space=pltpu.HBM),
                  pl.BlockSpec((tm, d), lambda i: (i, 0)),
                  pl.BlockSpec((tm, TOP_K), lambda i: (i, 0)),
                  pl.BlockSpec((None, 1, d), lambda i: (jnp.minimum(i // tpg, ng - 1), 0, 0))],
        out_specs=pl.BlockSpec((tm, d), lambda i: (i, 0)),
        out_shape=jax.ShapeDtypeStruct((t, d), F32),
        scratch_shapes=[buf, buf, pltpu.SemaphoreType.DMA(()), pltpu.SemaphoreType.DMA(())],
        compiler_params=_cparams("arbitrary"),
        name="combine",
    )(dest3, dest3, y_rt, x, top_g, gt2)


def _final_norm_kernel(x_ref, g_ref, o_ref):
    x = x_ref[...]
    o_ref[...] = x * lax.rsqrt(jnp.mean(x * x, axis=-1, keepdims=True) + EPS) * g_ref[...]


def _final_norm(x, g, tm=1024):
    t, d = x.shape
    return pl.pallas_call(
        _final_norm_kernel,
        grid=(t // tm,),
        in_specs=[pl.BlockSpec((tm, d), lambda i: (i, 0)), pl.BlockSpec((1, d), lambda i: (0, 0))],
        out_specs=pl.BlockSpec((tm, d), lambda i: (i, 0)),
        out_shape=jax.ShapeDtypeStruct((t, d), F32),
        compiler_params=_cparams("parallel"),
        name="final_norm",
    )(x, g.reshape(1, d))


def _slot_tables(cnt_row, n_tok):
    n_blocks = -(-n_tok * TOP_K // MOE_BLOCK) + N_EXPERTS
    counts = cnt_row[0, :N_EXPERTS].astype(jnp.int32)
    padded = (counts + MOE_BLOCK - 1) // MOE_BLOCK * MOE_BLOCK
    padded_end = jnp.cumsum(padded)
    pstart = padded_end - padded
    blk_start = jnp.arange(n_blocks, dtype=jnp.int32) * MOE_BLOCK
    block_e = jnp.minimum(jnp.sum(padded_end[None, :] <= blk_start[:, None], axis=1), N_EXPERTS - 1).astype(jnp.int32)
    n_used = (padded_end[-1:] // MOE_BLOCK).astype(jnp.int32)
    return counts, pstart.astype(jnp.int32), padded.astype(jnp.int32), block_e, n_used, n_blocks


def _slot_of(top_i, rank, pstart):
    e = jnp.arange(N_EXPERTS, dtype=jnp.int32)
    return rank + jnp.sum(jnp.where(top_i[..., None] == e, pstart, 0), axis=-1)


def _gate_tables(gb, bsz, seq_len):
    g5 = gb.reshape(bsz, seq_len, 2, 2, DN_HEADS)
    gbh = g5.transpose(0, 4, 1, 2, 3).reshape(bsz, DN_HEADS, seq_len, 4)
    grow = g5[:, :, :, 0, :].transpose(0, 3, 2, 1).reshape(bsz, DN_HEADS, 2, seq_len // GROUP, GROUP)
    return gbh, grow


def kernel(x, c, ctx, c_ctx, w_mod, b_mod, g_norm1, w_in, conv_w, a_log, dt_bias, g_out_norm, w_out, g_norm2,
           w_router, b_router, w_gate, b_gate, w_up, b_up, w_down, b_down, g_final):
    bsz, seq, d = x.shape
    clen = ctx.shape[1]
    depth = w_mod.shape[0]
    fw = FOURIER_GROUPS * LANES
    nmain = fw + 4 * DN_HEADS * DN_HEAD_DIM
    nab = 4 * DN_HEADS

    xt = x.reshape(bsz * seq, d)
    ct = ctx.reshape(bsz * clen, d)
    c_rows = jnp.concatenate([c, c_ctx[None, :], jnp.zeros((2 * SUBLANES - bsz - 1, d), F32)], axis=0)

    cp_x, sp_x = _dft_tables(seq)
    cp_c, sp_c = _dft_tables(clen)
    ch = np.arange(LANES, dtype=np.int64)
    ang = 2.0 * np.pi * ((ch[:, None] * ch[None, :]) % LANES) / LANES

    def chan(tab, n):
        return jnp.asarray(tab / math.sqrt(n * LANES), F32).astype(BF16)

    lanepad = lambda v: jnp.pad(v, ((0, 0), (0, LANES - v.shape[1])))
    zeros4 = jnp.zeros((DN_HEADS,), F32)
    isdec_row = lanepad(jnp.tile(jnp.concatenate([jnp.ones((DN_HEADS,), F32), zeros4]), 2)[None, :])

    for l in range(depth):
        update_ctx = l < depth - 1
        mod = _mod(c_rows, w_mod[l], b_mod[l]).reshape(2 * SUBLANES, 6, 1, d)
        mx = lambda j: mod[:bsz, j]
        mc = lambda j: mod[bsz:bsz + 1, j]

        w_main = w_in[l][:, :nmain].astype(BF16)
        w_ab = lanepad(w_in[l][:, nmain:]).astype(BF16)
        alog_row = lanepad(jnp.concatenate([a_log[l][0], zeros4, a_log[l][1], zeros4])[None, :])
        dtb_row = lanepad(jnp.concatenate([dt_bias[l][0], zeros4, dt_bias[l][1], zeros4])[None, :])
        ip = lambda tok, sh, sc, rpg: _inproj(tok, g_norm1[l], sc, sh, w_main, w_ab, alog_row, dtb_row,
                                               isdec_row, rpg)
        f_x, qkv_x, z_x, gb_x = ip(xt, mx(0), mx(1), seq)
        f_c, qkv_c, z_c, gb_c = ip(ct, mc(0), mc(1), bsz * clen)

        gbh_c, grow_c = _gate_tables(gb_c, bsz, clen)
        gbh_x, grow_x = _gate_tables(gb_x, bsz, seq)
        s_zero = jnp.zeros((bsz, DN_HEADS, DN_HEAD_DIM, DN_HEAD_DIM), F32)
        ocf, ocb, s_f, s_b = _deltanet(qkv_c, conv_w[l], gbh_c, grow_c, s_zero, s_zero, bsz, clen)
        oxf, oxb, _, _ = _deltanet(qkv_x, conv_w[l], gbh_x, grow_x, s_f, s_b, bsz, seq)

        w_out_b = w_out[l].astype(BF16)
        w_r = lanepad(w_router[l])
        b_r = lanepad(b_router[l][None, :])
        four_x = _fourier(f_x, cp_x, sp_x, chan(np.cos(ang), seq), chan(np.sin(ang), seq), bsz, seq)
        cnt0 = jnp.zeros((1, LANES), F32)
        xn_x, h2_x, ti_x, tg_x, rk_x, cnt = _outproj(four_x, oxf, oxb, z_x, xt, g_out_norm[l], mx(2), mx(4), mx(3),
                                                      g_norm2[l], w_out_b, w_r, b_r, cnt0, seq)
        n_tok = bsz * seq
        if update_ctx:
            four_c = _fourier(f_c, cp_c, sp_c, chan(np.cos(ang), clen), chan(np.sin(ang), clen), bsz, clen)
            xn_c, h2_c, ti_c, tg_c, rk_c, cnt = _outproj(four_c, ocf, ocb, z_c, ct, g_out_norm[l], mc(2), mc(4),
                                                          mc(3), g_norm2[l], w_out_b, w_r, b_r, cnt, bsz * clen)
            n_tok += bsz * clen

        counts, pstart, padded, block_e, n_used, n_blocks = _slot_tables(cnt, n_tok)
        dest_x = _slot_of(ti_x, rk_x, pstart)
        streams = [(h2_x, dest_x)]
        if update_ctx:
            dest_c = _slot_of(ti_c, rk_c, pstart)
            streams.append((h2_c, dest_c))
        xs = _dispatch(streams, counts, pstart, padded, n_used, n_blocks)
        y_rt = _experts(xs, block_e, n_used, l, w_gate, b_gate, w_up, b_up, w_down, b_down)
        xt = _combine(xn_x, y_rt, dest_x, tg_x, mx(5), seq)
        if update_ctx:
            ct = _combine(xn_c, y_rt, dest_c, tg_c, mc(5), bsz * clen)

    return _final_norm(xt, g_final).reshape(bsz, seq, d)
```

```python
import functools
import math

import numpy as np
import jax
import jax.numpy as jnp
from jax import lax
from jax.experimental import pallas as pl
from jax.experimental.pallas import tpu as pltpu

FOURIER_GROUPS = 4
DN_HEADS = 4
DN_HEAD_DIM = 128
CONV_K = 5
N_EXPERTS = 32
TOP_K = 4
SWIGLU_LIMIT = 7.0
SWIGLU_ALPHA = 1.702
EPS = 1e-6

LANES = 128
SUBLANES = 8
V7X_VMEM_BYTES = 64 * 1024 * 1024
VMEM_LIMIT = V7X_VMEM_BYTES - 8 * 1024 * 1024
CHUNK = 64
GROUP = 4 * CHUNK
MOE_BLOCK = 256

F32 = jnp.float32
BF16 = jnp.bfloat16


def _cparams(*sem):
    return pltpu.CompilerParams(dimension_semantics=sem, vmem_limit_bytes=VMEM_LIMIT)


def _sigmoid(x):
    return 1.0 / (1.0 + jnp.exp(-x))


def _mod_kernel(c_ref, w_ref, b_ref, o_ref):
    c = c_ref[...]
    a = c * _sigmoid(c)
    o_ref[...] = jnp.dot(a, w_ref[...], preferred_element_type=F32) + b_ref[...]


def _mod(c_rows, w_mod, b_mod):
    r, d = c_rows.shape
    n = w_mod.shape[1]
    tn = n // 4
    return pl.pallas_call(
        _mod_kernel,
        grid=(n // tn,),
        in_specs=[pl.BlockSpec((r, d), lambda j: (0, 0)),
                  pl.BlockSpec((d, tn), lambda j: (0, j)),
                  pl.BlockSpec((1, tn), lambda j: (0, j))],
        out_specs=pl.BlockSpec((r, tn), lambda j: (0, j)),
        out_shape=jax.ShapeDtypeStruct((r, n), F32),
        compiler_params=_cparams("parallel"),
        name="mod",
    )(c_rows, w_mod, b_mod.reshape(1, n))


def _inproj_kernel(x_ref, xp_ref, xn_ref, g_ref, sc_ref, sh_ref, w_ref, wab_ref, cw_ref, alog_ref, dtb_ref,
                   isdec_ref, f_ref, qkv_ref, z_ref, gb_ref, *, fw, qw, seq_len):
    i = pl.program_id(0)
    tm = x_ref.shape[0]
    halo = SUBLANES

    def norm_mod(x):
        inv = lax.rsqrt(jnp.mean(x * x, axis=-1, keepdims=True) + EPS)
        return (x * inv * g_ref[...]) * (1.0 + sc_ref[...]) + sh_ref[...]

    h = norm_mod(x_ref[...])
    hb = h.astype(BF16)
    at_start = (i * tm) % seq_len == 0
    at_end = ((i + 1) * tm) % seq_len == 0
    hp = jnp.where(at_start, 0.0, norm_mod(xp_ref[...]))
    hn = jnp.where(at_end, 0.0, norm_mod(xn_ref[...]))
    hext = jnp.concatenate([hp, h, hn], axis=0).astype(BF16)
    f_ref[...] = jnp.dot(hb, w_ref[:, :fw], preferred_element_type=F32)
    n = tm + 2 * halo
    cwid = 2 * DN_HEAD_DIM
    for c in range(3 * qw // cwid):
        lo = c * cwid
        pre = jnp.dot(hext, w_ref[:, fw + lo:fw + lo + cwid], preferred_element_type=F32)
        acc = None
        for j in range(CONV_K):
            off = halo - CONV_K // 2 + j
            term = cw_ref[j:j + 1, lo:lo + cwid] * pltpu.roll(pre, n - off, 0)[:tm]
            acc = term if acc is None else acc + term
        y = acc * _sigmoid(acc)
        if lo < 2 * qw:
            scale = DN_HEAD_DIM ** -0.5 if lo < qw else 1.0
            segs = []
            for hh in range(cwid // DN_HEAD_DIM):
                ys = y[:, hh * DN_HEAD_DIM:(hh + 1) * DN_HEAD_DIM]
                segs.append(ys * (lax.rsqrt(jnp.sum(ys * ys, axis=-1, keepdims=True) + EPS) * scale))
            y = jnp.concatenate(segs, axis=1)
        qkv_ref[:, lo:lo + cwid] = y
    lo = fw + 3 * qw
    z_ref[...] = jnp.dot(hb, w_ref[:, lo:lo + qw], preferred_element_type=F32)
    ab = jnp.dot(hb, wab_ref[...], preferred_element_type=F32)
    pre = ab + dtb_ref[...]
    softplus = jnp.maximum(pre, 0.0) + jnp.log(1.0 + jnp.exp(-jnp.abs(pre)))
    dec = -jnp.exp(alog_ref[...]) * softplus
    gb = jnp.where(isdec_ref[...] > 0.0, dec, _sigmoid(ab))
    gb_ref[...] = gb[:, :gb_ref.shape[1]]


def _inproj(x, g1, sc, sh, w_main, w_ab, conv_w, alog_row, dtb_row, isdec_row, rows_per_group, seq_len):
    t, d = x.shape
    ng = sc.shape[0]
    fw = FOURIER_GROUPS * LANES
    qw = DN_HEADS * DN_HEAD_DIM
    nab = 4 * DN_HEADS
    tm = min(512, seq_len)
    hpt = tm // SUBLANES
    tpg = rows_per_group // tm
    grp = lambda i: (jnp.minimum(i // tpg, ng - 1), 0, 0)
    const = lambda i: (0, 0)
    return pl.pallas_call(
        functools.partial(_inproj_kernel, fw=fw, qw=qw, seq_len=seq_len),
        grid=(t // tm,),
        in_specs=[pl.BlockSpec((tm, d), lambda i: (i, 0)),
                  pl.BlockSpec((SUBLANES, d), lambda i: (jnp.maximum(i * hpt - 1, 0), 0)),
                  pl.BlockSpec((SUBLANES, d), lambda i: (jnp.minimum((i + 1) * hpt, t // SUBLANES - 1), 0)),
                  pl.BlockSpec((1, d), const),
                  pl.BlockSpec((None, 1, d), grp),
                  pl.BlockSpec((None, 1, d), grp),
                  pl.BlockSpec(w_main.shape, const),
                  pl.BlockSpec(w_ab.shape, const),
                  pl.BlockSpec(conv_w.shape, const),
                  pl.BlockSpec((1, LANES), const),
                  pl.BlockSpec((1, LANES), const),
                  pl.BlockSpec((1, LANES), const)],
        out_specs=[pl.BlockSpec((tm, fw), lambda i: (i, 0)),
                   pl.BlockSpec((tm, 3 * qw), lambda i: (i, 0)),
                   pl.BlockSpec((tm, qw), lambda i: (i, 0)),
                   pl.BlockSpec((tm, nab), lambda i: (i, 0))],
        out_shape=[jax.ShapeDtypeStruct((t, fw), F32),
                   jax.ShapeDtypeStruct((t, 3 * qw), F32),
                   jax.ShapeDtypeStruct((t, qw), F32),
                   jax.ShapeDtypeStruct((t, nab), F32)],
        compiler_params=_cparams("parallel"),
        name="inproj",
    )(x, x, x, g1.reshape(1, d), sc, sh, w_main, w_ab, conv_w, alog_row, dtb_row, isdec_row)


def _bdot(a, b):
    return jnp.dot(a.astype(BF16), b.astype(BF16), preferred_element_type=F32)


def _dot_nt(a, b):
    return lax.dot_general(a.astype(BF16), b.astype(BF16), (((1,), (1,)), ((), ())), preferred_element_type=F32)


def _dot_tn(a, b):
    return lax.dot_general(a, b, (((0,), (0,)), ((), ())), preferred_element_type=F32)


def _dn_masks(dirn):
    ii = lax.broadcasted_iota(jnp.int32, (GROUP, GROUP), 0)
    jj = lax.broadcasted_iota(jnp.int32, (GROUP, GROUP), 1)
    same = (ii // CHUNK) == (jj // CHUNK)
    if dirn == 0:
        return same, same & (ii >= jj), same & (ii > jj), same & (ii <= jj)
    return same, same & (ii <= jj), same & (ii < jj), same & (ii >= jj)


def _dn_groups(chains, states, q_s, k_s, v_s, gb_ref, gr_ref, o_refs):
    rng = range(len(chains))
    dh = DN_HEAD_DIM
    masks = {d: _dn_masks(d) for d in sorted({c[1] for c in chains})}
    same = [masks[c[1]][0] for c in chains]

    def pack(bd):
        return bd[0:CHUNK] + bd[CHUNK:2 * CHUNK] + bd[2 * CHUNK:3 * CHUNK] + bd[3 * CHUNK:]

    def unpack(pk, sm):
        return jnp.where(sm, jnp.concatenate([pk, pk, pk, pk], axis=0), 0.0)

    r0s, q, k, v, bcol, gam_col, tot_col, dec = [], [], [], [], [], [], [], []
    for hh, dirn, g in chains:
        r0 = pl.multiple_of(g * GROUP, GROUP)
        cols = slice(hh * dh, (hh + 1) * dh)
        r0s.append(r0)
        q.append(q_s[pl.ds(r0, GROUP), cols])
        k.append(k_s[pl.ds(r0, GROUP), cols])
        v.append(v_s[pl.ds(r0, GROUP), cols])
        cg = 2 * DN_HEADS * dirn + hh
        gcol = gb_ref[pl.ds(r0, GROUP), cg:cg + 1]
        bcol.append(gb_ref[pl.ds(r0, GROUP), cg + DN_HEADS:cg + DN_HEADS + 1])
        grow = gr_ref[hh, dirn, pl.ds(g, 1), :]
        sm, incl, _, incl_t = masks[dirn]
        gc = jnp.sum(jnp.where(incl, grow, 0.0), axis=1, keepdims=True)
        gr = jnp.sum(jnp.where(incl_t, gcol, 0.0), axis=0, keepdims=True)
        gam_col.append(gc)
        tot_col.append(jnp.sum(jnp.where(sm, grow, 0.0), axis=1, keepdims=True))
        dec.append(jnp.where(incl, jnp.exp(jnp.where(incl, gc - gr, 0.0)), 0.0))

    kk = [_dot_nt(k[i], k[i]) for i in rng]
    qk = [_dot_nt(q[i], k[i]) for i in rng]
    mk_bd = [jnp.where(masks[chains[i][1]][2], -(bcol[i] * kk[i] * dec[i]), 0.0) for i in rng]
    attn = [qk[i] * dec[i] for i in rng]

    pi = lax.broadcasted_iota(jnp.int32, (CHUNK, GROUP), 0)
    pj = lax.broadcasted_iota(jnp.int32, (CHUNK, GROUP), 1)
    eye_pk = jnp.where(pi == (pj % CHUNK), 1.0, 0.0)
    mk_pk = [pack(mk_bd[i]) for i in rng]
    p = [eye_pk + mk_pk[i] for i in rng]
    mk_b = [mk_bd[i].astype(BF16) for i in rng]
    for _ in range(5):
        mk_pk = [jnp.dot(mk_pk[i].astype(BF16), mk_b[i], preferred_element_type=F32) for i in rng]
        mk_b = [unpack(mk_pk[i], same[i]).astype(BF16) for i in rng]
        p = [p[i] + jnp.dot(p[i].astype(BF16), mk_b[i], preferred_element_type=F32) for i in rng]
    t_bd = [unpack(p[i], same[i]) for i in rng]

    eg = [jnp.exp(gam_col[i]) for i in rng]
    rhs = [jnp.concatenate([v[i] * bcol[i], k[i] * (bcol[i] * eg[i])], axis=1) for i in rng]
    uw = [_bdot(t_bd[i], rhs[i]) for i in rng]
    qg = [q[i] * eg[i] for i in rng]
    kend = [k[i] * jnp.exp(tot_col[i] - gam_col[i]) for i in rng]
    gend = [jnp.exp(tot_col[i]) for i in rng]

    zeros = jnp.zeros((CHUNK, dh), F32)
    for step in range(GROUP // CHUNK):
        cs = [step if chains[i][1] == 0 else GROUP // CHUNK - 1 - step for i in rng]
        lo = [c * CHUNK for c in cs]
        lhs = [jnp.concatenate([uw[i][lo[i]:lo[i] + CHUNK, dh:], qg[i][lo[i]:lo[i] + CHUNK]], axis=0)
               for i in rng]
        wq = [_bdot(lhs[i], states[i]) for i in rng]
        v_new = [uw[i][lo[i]:lo[i] + CHUNK, :dh] - wq[i][:CHUNK] for i in rng]
        states = [states[i] * gend[i][lo[i]:lo[i] + 1] + _dot_tn(kend[i][lo[i]:lo[i] + CHUNK], v_new[i])
                  for i in rng]
        for i in rng:
            v_full = jnp.concatenate([v_new[i] if cc == cs[i] else zeros for cc in range(GROUP // CHUNK)], axis=0)
            o_c = wq[i][CHUNK:] + _bdot(attn[i][lo[i]:lo[i] + CHUNK], v_full)
            hh = chains[i][0]
            o_refs[chains[i][1]][pl.ds(r0s[i] + lo[i], CHUNK), hh * dh:(hh + 1) * dh] = o_c
    return states


def _dn_kernel(q_s, k_s, v_s, gb_ref, gr_ref, s0f_ref, s0b_ref, of_ref, ob_ref, sf_ref, sb_ref, *, seq_len):
    ng = seq_len // GROUP
    nh = DN_HEADS

    def body(m, carry):
        chains = [(hh, d, m if d == 0 else ng - 1 - m) for hh in range(nh) for d in (0, 1)]
        return tuple(_dn_groups(chains, list(carry), q_s, k_s, v_s, gb_ref, gr_ref, (of_ref, ob_ref)))

    init = tuple(r[hh] for hh in range(nh) for r in (s0f_ref, s0b_ref))
    fin = lax.fori_loop(0, ng, body, init)
    for hh in range(nh):
        sf_ref[hh] = fin[2 * hh]
        sb_ref[hh] = fin[2 * hh + 1]


def _deltanet(qkv, gb, grow, s0f, s0b, bsz, seq_len):
    h, dh = DN_HEADS, DN_HEAD_DIM
    ng = seq_len // GROUP
    w = h * dh
    once = pl.Buffered(1)
    col = lambda j: pl.BlockSpec((seq_len, w), lambda b: (b, j), pipeline_mode=once)
    st = pl.BlockSpec((None, h, dh, dh), lambda b: (b, 0, 0, 0))
    return pl.pallas_call(
        functools.partial(_dn_kernel, seq_len=seq_len),
        grid=(bsz,),
        in_specs=[col(0), col(1), col(2),
                  pl.BlockSpec((seq_len, 4 * h), lambda b: (b, 0), pipeline_mode=once),
                  pl.BlockSpec((None, h, 2, ng, GROUP), lambda b: (b, 0, 0, 0, 0)),
                  st, st],
        out_specs=[col(0), col(0), st, st],
        out_shape=[jax.ShapeDtypeStruct((bsz * seq_len, w), F32),
                   jax.ShapeDtypeStruct((bsz * seq_len, w), F32),
                   jax.ShapeDtypeStruct((bsz, h, dh, dh), F32),
                   jax.ShapeDtypeStruct((bsz, h, dh, dh), F32)],
        compiler_params=_cparams("parallel"),
        name="deltanet",
    )(qkv, qkv, qkv, gb, grow, s0f, s0b)


def _dft_table_kernel(tac_ref, tas_ref, tbc_ref, tbs_ref, cp_ref, sp_ref, *, nt1):
    tbc = tbc_ref[...]
    tbs = tbs_ref[...]
    for t1 in range(nt1):
        ac = tac_ref[:, t1:t1 + 1]
        asn = tas_ref[:, t1:t1 + 1]
        cp_ref[:, t1 * LANES:(t1 + 1) * LANES] = (ac * tbc - asn * tbs).astype(BF16)
        sp_ref[:, t1 * LANES:(t1 + 1) * LANES] = (asn * tbc + ac * tbs).astype(BF16)


def _dft_tables(n):
    nt1 = n // LANES
    s = np.arange(n, dtype=np.int64)[:, None]
    t1 = np.arange(nt1, dtype=np.int64)[None, :]
    t0 = np.arange(LANES, dtype=np.int64)[None, :]
    ang_a = 2.0 * np.pi * ((s * t1) % nt1) / nt1
    ang_b = 2.0 * np.pi * ((s * t0) % n) / n
    pad = ((0, 0), (0, LANES - nt1))
    tac = jnp.asarray(np.pad(np.cos(ang_a), pad), F32)
    tas = jnp.asarray(np.pad(np.sin(ang_a), pad), F32)
    tbc = jnp.asarray(np.cos(ang_b), F32)
    tbs = jnp.asarray(np.sin(ang_b), F32)
    tr = min(n, 256)
    small = pl.BlockSpec((tr, LANES), lambda i: (i, 0))
    big = pl.BlockSpec((tr, n), lambda i: (i, 0))
    return pl.pallas_call(
        functools.partial(_dft_table_kernel, nt1=nt1),
        grid=(n // tr,),
        in_specs=[small, small, small, small],
        out_specs=[big, big],
        out_shape=[jax.ShapeDtypeStruct((n, n), BF16)] * 2,
        compiler_params=_cparams("parallel"),
        name="dft_tables",
    )(tac, tas, tbc, tbs)


def _fourier_kernel(f_ref, cp_ref, sp_ref, cc_ref, sc_ref, o_ref):
    kstep = pl.program_id(1)
    xb = f_ref[...].astype(BF16)
    cc = cc_ref[...]
    sc = sc_ref[...]
    zc = [jnp.dot(xb[:, g * LANES:(g + 1) * LANES], cc, preferred_element_type=F32) for g in range(FOURIER_GROUPS)]
    zs = [jnp.dot(xb[:, g * LANES:(g + 1) * LANES], sc, preferred_element_type=F32) for g in range(FOURIER_GROUPS)]
    zc = jnp.concatenate(zc, axis=1).astype(BF16)
    zs = jnp.concatenate(zs, axis=1).astype(BF16)
    part = (jnp.dot(cp_ref[...], zc, preferred_element_type=F32)
            - jnp.dot(sp_ref[...], zs, preferred_element_type=F32))

    @pl.when(kstep == 0)
    def _():
        o_ref[...] = part

    @pl.when(kstep > 0)
    def _():
        o_ref[...] += part


def _fourier(f, cp, sp, cc, sc, bsz, seq_len):
    fw = f.shape[1]
    tk = min(seq_len, 512)
    return pl.pallas_call(
        _fourier_kernel,
        grid=(bsz, seq_len // tk),
        in_specs=[pl.BlockSpec((tk, fw), lambda b, k: (b * (seq_len // tk) + k, 0)),
                  pl.BlockSpec((seq_len, tk), lambda b, k: (0, k)),
                  pl.BlockSpec((seq_len, tk), lambda b, k: (0, k)),
                  pl.BlockSpec((LANES, LANES), lambda b, k: (0, 0)),
                  pl.BlockSpec((LANES, LANES), lambda b, k: (0, 0))],
        out_specs=pl.BlockSpec((seq_len, fw), lambda b, k: (b, 0)),
        out_shape=jax.ShapeDtypeStruct((bsz * seq_len, fw), F32),
        compiler_params=_cparams("parallel", "arbitrary"),
        name="fourier",
    )(f, cp, sp, cc, sc)


def _outproj_kernel(four_ref, of_ref, ob_ref, z_ref, x_ref, gout_ref, gt1_ref, sc2_ref, sh2_ref, g2_ref,
                    wout_ref, wr_ref, br_ref, cnt0_ref, xn_ref, h2_ref, ti_ref, tg_ref, rk_ref, cnt_ref,
                    carry, *, fw):
    @pl.when(pl.program_id(0) == 0)
    def _():
        carry[...] = cnt0_ref[...]

    o = of_ref[...] + ob_ref[...]
    z = z_ref[...]
    gout = gout_ref[...]
    parts = []
    for hh in range(DN_HEADS):
        sl = slice(hh * DN_HEAD_DIM, (hh + 1) * DN_HEAD_DIM)
        oh = o[:, sl]
        zh = z[:, sl]
        oh = oh * lax.rsqrt(jnp.mean(oh * oh, axis=-1, keepdims=True) + EPS) * gout
        parts.append(oh * (zh * _sigmoid(zh)))
    gated = jnp.concatenate(parts, axis=1).astype(BF16)
    mix = (jnp.dot(four_ref[...].astype(BF16), wout_ref[:fw, :], preferred_element_type=F32)
           + jnp.dot(gated, wout_ref[fw:, :], preferred_element_type=F32))
    xn = x_ref[...] + gt1_ref[...] * mix
    xn_ref[...] = xn
    inv = lax.rsqrt(jnp.mean(xn * xn, axis=-1, keepdims=True) + EPS)
    h2 = (xn * inv * g2_ref[...]) * (1.0 + sc2_ref[...]) + sh2_ref[...]
    tm = h2.shape[0]
    for s in range(h2.shape[1] // LANES):
        h2_ref[pl.ds(s, tm, stride=SUBLANES), :] = h2[:, s * LANES:(s + 1) * LANES]
    wr = wr_ref[...]
    wr_hi = wr.astype(BF16)
    wr_lo = (wr - wr_hi.astype(F32)).astype(BF16)
    h2_hi = h2.astype(BF16)
    h2_lo = (h2 - h2_hi.astype(F32)).astype(BF16)
    logits = (jnp.dot(h2_hi, wr_hi, preferred_element_type=F32) + jnp.dot(h2_hi, wr_lo, preferred_element_type=F32)
              + jnp.dot(h2_lo, wr_hi, preferred_element_type=F32)) + br_ref[...]
    lane = lax.broadcasted_iota(jnp.int32, logits.shape, 1)
    neg = jnp.float32(-jnp.inf)
    logits = jnp.where(lane < N_EXPERTS, logits, neg)
    vals, idxs = [], []
    for _ in range(TOP_K):
        m = jnp.max(logits, axis=-1, keepdims=True)
        idx = jnp.min(jnp.where(logits == m, lane, LANES), axis=-1, keepdims=True)
        vals.append(m)
        idxs.append(idx)
        logits = jnp.where(lane == idx, neg, logits)
    ex = [jnp.exp(vv - vals[0]) for vv in vals]
    den = ex[0] + ex[1] + ex[2] + ex[3]
    onehot = jnp.zeros(logits.shape, F32)
    for j in range(TOP_K):
        onehot = onehot + jnp.where(lane == idxs[j], 1.0, 0.0)
    ri = lax.broadcasted_iota(jnp.int32, (tm, tm), 0)
    ci = lax.broadcasted_iota(jnp.int32, (tm, tm), 1)
    tri = jnp.where(ri > ci, 1.0, 0.0).astype(BF16)
    base = jnp.dot(tri, onehot.astype(BF16), preferred_element_type=F32) + carry[...]
    carry[...] = carry[...] + jnp.sum(onehot, axis=0, keepdims=True)
    cnt_ref[...] = carry[...]
    kl = lax.broadcasted_iota(jnp.int32, (tm, TOP_K), 1)
    ti = jnp.zeros((tm, TOP_K), jnp.int32)
    tg = jnp.zeros((tm, TOP_K), F32)
    rk = jnp.zeros((tm, TOP_K), jnp.int32)
    for j in range(TOP_K):
        rank_j = jnp.sum(jnp.where(lane == idxs[j], base, 0.0), axis=-1, keepdims=True).astype(jnp.int32)
        ti = jnp.where(kl == j, idxs[j], ti)
        tg = jnp.where(kl == j, ex[j] / den, tg)
        rk = jnp.where(kl == j, rank_j, rk)
    ti_ref[...] = ti
    tg_ref[...] = tg
    rk_ref[...] = rk


def _outproj(four, o_f, o_b, z, x, g_out, gt1, sc2, sh2, g2, w_out, w_r, b_r, cnt0, rows_per_group, tm=256):
    t, d = x.shape
    fw = four.shape[1]
    ng = gt1.shape[0]
    tpg = rows_per_group // tm
    grp = lambda i: (jnp.minimum(i // tpg, ng - 1), 0, 0)
    const = lambda i: (0, 0)
    row = lambda w: pl.BlockSpec((tm, w), lambda i: (i, 0))
    gspec = pl.BlockSpec((None, 1, d), grp)
    return pl.pallas_call(
        functools.partial(_outproj_kernel, fw=fw),
        grid=(t // tm,),
        in_specs=[row(fw), row(z.shape[1]), row(z.shape[1]), row(z.shape[1]), row(d),
                  pl.BlockSpec((1, DN_HEAD_DIM), const), gspec, gspec, gspec,
                  pl.BlockSpec((1, d), const),
                  pl.BlockSpec(w_out.shape, const),
                  pl.BlockSpec(w_r.shape, const),
                  pl.BlockSpec((1, LANES), const),
                  pl.BlockSpec((1, LANES), const)],
        out_specs=[row(d),
                   pl.BlockSpec((tm * SUBLANES, LANES), lambda i: (i, 0)),
                   row(TOP_K), row(TOP_K), row(TOP_K),
                   pl.BlockSpec((1, LANES), const)],
        out_shape=[jax.ShapeDtypeStruct((t, d), F32),
                   jax.ShapeDtypeStruct((t * d // LANES, LANES), F32),
                   jax.ShapeDtypeStruct((t, TOP_K), jnp.int32),
                   jax.ShapeDtypeStruct((t, TOP_K), F32),
                   jax.ShapeDtypeStruct((t, TOP_K), jnp.int32),
                   jax.ShapeDtypeStruct((1, LANES), F32)],
        scratch_shapes=[pltpu.VMEM((1, LANES), F32)],
        compiler_params=_cparams("arbitrary"),
        name="outproj",
    )(four, o_f, o_b, z, x, g_out.reshape(1, DN_HEAD_DIM), gt1, sc2, sh2, g2.reshape(1, d), w_out, w_r, b_r, cnt0)


TOK_UNROLL = 4


def _row_copy(src, src_row, dst, dst_row, sem):
    return pltpu.make_async_copy(
        src.at[pl.ds(pl.multiple_of(src_row * SUBLANES, SUBLANES), SUBLANES), :],
        dst.at[pl.ds(pl.multiple_of(dst_row * SUBLANES, SUBLANES), SUBLANES), :], sem)


def _dispatch_kernel(cnt_ref, ps_ref, pd_ref, nu_ref, *refs, tm, steps, n_blocks):
    ns = len(steps)
    dest_refs, h_refs = refs[0:2 * ns:2], refs[1:2 * ns:2]
    xs_hbm, zero_s, sem, sem_pad = refs[2 * ns:]
    i = pl.program_id(0)
    blk_rows = MOE_BLOCK * SUBLANES

    def tail_copy(b):
        return pltpu.make_async_copy(zero_s, xs_hbm.at[pl.ds(pl.multiple_of(b * blk_rows, blk_rows), blk_rows), :],
                                     sem_pad)

    @pl.when(i == 0)
    def _():
        zero_s[...] = jnp.zeros(zero_s.shape, F32)

        def per_expert(e, tot):
            def one(r, c):
                _row_copy(zero_s, 0, xs_hbm, ps_ref[e] + r, sem_pad).start()
                return c

            lax.fori_loop(cnt_ref[e], pd_ref[e], one, 0)
            return tot + pd_ref[e] - cnt_ref[e]

        n_pad = lax.fori_loop(0, N_EXPERTS, per_expert, 0)

        def start_tail(b, c):
            tail_copy(b).start()
            return c

        lax.fori_loop(nu_ref[0], n_blocks, start_tail, 0)

        def wait_row(r, c):
            _row_copy(zero_s, 0, xs_hbm, 0, sem_pad).wait()
            return c

        lax.fori_loop(0, n_pad, wait_row, 0)

        def wait_tail(b, c):
            tail_copy(b).wait()
            return c

        lax.fori_loop(nu_ref[0], n_blocks, wait_tail, 0)

    first = 0
    for dest_ref, h_ref, nst in zip(dest_refs, h_refs, steps):
        @pl.when((i >= first) & (i < first + nst))
        def _(dest_ref=dest_ref, h_ref=h_ref):
            def issue(blk, carry):
                for tt in range(TOK_UNROLL):
                    t = blk * TOK_UNROLL + tt
                    for j in range(TOP_K):
                        _row_copy(h_ref, t, xs_hbm, dest_ref[0, 0, t * TOP_K + j], sem).start(priority=j % 2)
                return carry

            lax.fori_loop(0, tm // TOK_UNROLL, issue, 0)
            for _ in range(TOP_K):
                pltpu.make_async_copy(h_ref, xs_hbm.at[pl.ds(0, tm * SUBLANES), :], sem).wait()

        first += nst


def _dispatch(streams, counts, pstart, padded, n_used, n_blocks, tm=256):
    seg = streams[0][0].shape[0] // streams[0][1].shape[0]
    steps = [dest.shape[0] // tm for _, dest in streams]
    in_specs, args, first = [], [], 0
    for (h_rt, dest), nst in zip(streams, steps):
        idx = lambda i, *_, first=first, nst=nst: (jnp.clip(i - first, 0, nst - 1), 0)
        idx3 = lambda i, *_, idx=idx: idx(i) + (0,)
        in_specs += [pl.BlockSpec((1, 1, tm * TOP_K), idx3, memory_space=pltpu.SMEM),
                     pl.BlockSpec((tm * seg, LANES), idx)]
        args += [dest.reshape(nst, 1, tm * TOP_K), h_rt]
        first += nst
    grid_spec = pltpu.PrefetchScalarGridSpec(
        num_scalar_prefetch=4,
        grid=(first,),
        in_specs=in_specs,
        out_specs=pl.BlockSpec(memory_space=pltpu.HBM),
        scratch_shapes=[pltpu.VMEM((MOE_BLOCK * seg, LANES), F32),
                        pltpu.SemaphoreType.DMA(()), pltpu.SemaphoreType.DMA(())],
    )
    return pl.pallas_call(
        functools.partial(_dispatch_kernel, tm=tm, steps=tuple(steps), n_blocks=n_blocks),
        grid_spec=grid_spec,
        out_shape=jax.ShapeDtypeStruct((n_blocks * MOE_BLOCK * seg, LANES), F32),
        compiler_params=_cparams("arbitrary"),
        name="dispatch",
    )(counts, pstart, padded, n_used, *args)


def _expert_kernel(be_ref, nu_ref, xs_ref, wg_ref, bg_ref, wu_ref, bu_ref, wd_ref, bd_ref, y_ref,
                   x2d, wg_b, wu_b, wd_b):
    i = pl.program_id(0)
    rows = MOE_BLOCK
    nseg = x2d.shape[1] // LANES

    @pl.when(i < nu_ref[0])
    def _():
        @pl.when((i == 0) | (be_ref[i] != be_ref[jnp.maximum(i - 1, 0)]))
        def _():
            wg_b[...] = wg_ref[...].astype(BF16)
            wu_b[...] = wu_ref[...].astype(BF16)
            wd_b[...] = wd_ref[...].astype(BF16)

        for s in range(nseg):
            x2d[:, s * LANES:(s + 1) * LANES] = xs_ref[pl.ds(s, rows, stride=SUBLANES), :].astype(BF16)
        x = x2d[...]
        a = jnp.minimum(jnp.dot(x, wg_b[...], preferred_element_type=F32) + bg_ref[...], SWIGLU_LIMIT)
        u = jnp.clip(jnp.dot(x, wu_b[...], preferred_element_type=F32) + bu_ref[...], -SWIGLU_LIMIT, SWIGLU_LIMIT)
        act = (a * _sigmoid(SWIGLU_ALPHA * a) * (u + 1.0)).astype(BF16)
        y = jnp.dot(act, wd_b[...], preferred_element_type=F32) + bd_ref[...]
        for s in range(nseg):
            y_ref[pl.ds(s, rows, stride=SUBLANES), :] = y[:, s * LANES:(s + 1) * LANES]

    @pl.when(i >= nu_ref[0])
    def _():
        y_ref[...] = jnp.zeros(y_ref.shape, F32)


def _experts(xs, block_e, n_used, layer, w_gate, b_gate, w_up, b_up, w_down, b_down):
    n_blocks = block_e.shape[0]
    depth, ne, d, ff = w_gate.shape
    seg = d // LANES
    blk = lambda i, nu: jnp.minimum(i, jnp.maximum(nu[0] - 1, 0))
    wspec = lambda shp: pl.BlockSpec((None, None) + shp, lambda i, be, nu: (layer, be[blk(i, nu)], 0, 0))
    grid_spec = pltpu.PrefetchScalarGridSpec(
        num_scalar_prefetch=2,
        grid=(n_blocks,),
        in_specs=[pl.BlockSpec((MOE_BLOCK * seg, LANES), lambda i, be, nu: (blk(i, nu), 0)),
                  wspec((d, ff)), wspec((1, ff)), wspec((d, ff)), wspec((1, ff)), wspec((ff, d)), wspec((1, d))],
        out_specs=pl.BlockSpec((MOE_BLOCK * seg, LANES), lambda i, be, nu: (i, 0)),
        scratch_shapes=[pltpu.VMEM((MOE_BLOCK, d), BF16), pltpu.VMEM((d, ff), BF16), pltpu.VMEM((d, ff), BF16),
                        pltpu.VMEM((ff, d), BF16)],
    )
    return pl.pallas_call(
        _expert_kernel,
        grid_spec=grid_spec,
        out_shape=jax.ShapeDtypeStruct((n_blocks * MOE_BLOCK * seg, LANES), F32),
        compiler_params=_cparams("arbitrary"),
        name="experts",
    )(block_e, n_used, xs, w_gate, b_gate.reshape(depth, ne, 1, ff), w_up, b_up.reshape(depth, ne, 1, ff),
      w_down, b_down.reshape(depth, ne, 1, d))


def _combine_kernel(dest_ref, dnext_ref, y_hbm, x_ref, tg_ref, gt_ref, o_ref, buf0, buf1, sem0, sem1, *, tm):
    n = tm * TOP_K
    i = pl.program_id(0)
    last = pl.num_programs(0) - 1

    def gather(d_ref, buf, sem):
        def issue(blk, carry):
            for tt in range(TOK_UNROLL):
                t = blk * TOK_UNROLL + tt
                for j in range(TOP_K):
                    _row_copy(y_hbm, d_ref[0, 0, t * TOP_K + j], buf, j * tm + t, sem).start(priority=j % 2)
            return carry

        lax.fori_loop(0, tm // TOK_UNROLL, issue, 0)

    def reduce(buf, sem):
        pltpu.make_async_copy(y_hbm.at[pl.ds(0, n * SUBLANES), :], buf, sem).wait()
        gt = gt_ref[...]
        tg = tg_ref[...]
        for s in range(o_ref.shape[1] // LANES):
            acc = tg[:, 0:1] * buf[pl.ds(s, tm, stride=SUBLANES), :]
            for j in range(1, TOP_K):
                acc = acc + tg[:, j:j + 1] * buf[pl.ds(j * tm * SUBLANES + s, tm, stride=SUBLANES), :]
            sl = slice(s * LANES, (s + 1) * LANES)
            o_ref[:, sl] = x_ref[:, sl] + gt[:, sl] * acc

    @pl.when(i == 0)
    def _():
        gather(dest_ref, buf0, sem0)

    for par, (cur, nxt) in enumerate((((buf0, sem0), (buf1, sem1)), ((buf1, sem1), (buf0, sem0)))):
        @pl.when(i % 2 == par)
        def _(cur=cur, nxt=nxt):
            @pl.when(i < last)
            def _():
                gather(dnext_ref, *nxt)

            reduce(*cur)


def _combine(x, y_rt, dest, top_g, gt2, rows_per_group, tm=128):
    t, d = x.shape
    ng = gt2.shape[0]
    nt = t // tm
    tpg = rows_per_group // tm
    dest3 = dest.reshape(nt, 1, tm * TOP_K)
    buf = pltpu.VMEM((tm * TOP_K * SUBLANES, LANES), F32)
    return pl.pallas_call(
        functools.partial(_combine_kernel, tm=tm),
        grid=(nt,),
        in_specs=[pl.BlockSpec((1, 1, tm * TOP_K), lambda i: (i, 0, 0), memory_space=pltpu.SMEM),
                  pl.BlockSpec((1, 1, tm * TOP_K), lambda i: (jnp.minimum(i + 1, nt - 1), 0, 0),
                               memory_space=pltpu.SMEM),
                  pl.BlockSpec(memory_space=pltpu.HBM),
                  pl.BlockSpec((tm, d), lambda i: (i, 0)),
                  pl.BlockSpec((tm, TOP_K), lambda i: (i, 0)),
                  pl.BlockSpec((None, 1, d), lambda i: (jnp.minimum(i // tpg, ng - 1), 0, 0))],
        out_specs=pl.BlockSpec((tm, d), lambda i: (i, 0)),
        out_shape=jax.ShapeDtypeStruct((t, d), F32),
        scratch_shapes=[buf, buf, pltpu.SemaphoreType.DMA(()), pltpu.SemaphoreType.DMA(())],
        compiler_params=_cparams("arbitrary"),
        name="combine",
    )(dest3, dest3, y_rt, x, top_g, gt2)


def _final_norm_kernel(x_ref, g_ref, o_ref):
    x = x_ref[...]
    o_ref[...] = x * lax.rsqrt(jnp.mean(x * x, axis=-1, keepdims=True) + EPS) * g_ref[...]


def _final_norm(x, g, tm=1024):
    t, d = x.shape
    return pl.pallas_call(
        _final_norm_kernel,
        grid=(t // tm,),
        in_specs=[pl.BlockSpec((tm, d), lambda i: (i, 0)), pl.BlockSpec((1, d), lambda i: (0, 0))],
        out_specs=pl.BlockSpec((tm, d), lambda i: (i, 0)),
        out_shape=jax.ShapeDtypeStruct((t, d), F32),
        compiler_params=_cparams("parallel"),
        name="final_norm",
    )(x, g.reshape(1, d))


def _slot_tables(cnt_row, n_tok):
    n_blocks = -(-n_tok * TOP_K // MOE_BLOCK) + N_EXPERTS
    counts = cnt_row[0, :N_EXPERTS].astype(jnp.int32)
    padded = (counts + MOE_BLOCK - 1) // MOE_BLOCK * MOE_BLOCK
    padded_end = jnp.cumsum(padded)
    pstart = padded_end - padded
    blk_start = jnp.arange(n_blocks, dtype=jnp.int32) * MOE_BLOCK
    block_e = jnp.minimum(jnp.sum(padded_end[None, :] <= blk_start[:, None], axis=1), N_EXPERTS - 1).astype(jnp.int32)
    n_used = (padded_end[-1:] // MOE_BLOCK).astype(jnp.int32)
    return counts, pstart.astype(jnp.int32), padded.astype(jnp.int32), block_e, n_used, n_blocks


def _slot_of(top_i, rank, pstart):
    e = jnp.arange(N_EXPERTS, dtype=jnp.int32)
    return rank + jnp.sum(jnp.where(top_i[..., None] == e, pstart, 0), axis=-1)


def _decay_rows(gb, bsz, seq_len):
    g5 = gb.reshape(bsz, seq_len, 2, 2, DN_HEADS)
    return g5[:, :, :, 0, :].transpose(0, 3, 2, 1).reshape(bsz, DN_HEADS, 2, seq_len // GROUP, GROUP)


def kernel(x, c, ctx, c_ctx, w_mod, b_mod, g_norm1, w_in, conv_w, a_log, dt_bias, g_out_norm, w_out, g_norm2,
           w_router, b_router, w_gate, b_gate, w_up, b_up, w_down, b_down, g_final):
    bsz, seq, d = x.shape
    clen = ctx.shape[1]
    depth = w_mod.shape[0]
    fw = FOURIER_GROUPS * LANES
    nmain = fw + 4 * DN_HEADS * DN_HEAD_DIM
    nab = 4 * DN_HEADS

    xt = x.reshape(bsz * seq, d)
    ct = ctx.reshape(bsz * clen, d)
    c_rows = jnp.concatenate([c, c_ctx[None, :], jnp.zeros((2 * SUBLANES - bsz - 1, d), F32)], axis=0)

    cp_x, sp_x = _dft_tables(seq)
    cp_c, sp_c = _dft_tables(clen)
    ch = np.arange(LANES, dtype=np.int64)
    ang = 2.0 * np.pi * ((ch[:, None] * ch[None, :]) % LANES) / LANES

    def chan(tab, n):
        return jnp.asarray(tab / math.sqrt(n * LANES), F32).astype(BF16)

    lanepad = lambda v: jnp.pad(v, ((0, 0), (0, LANES - v.shape[1])))
    zeros4 = jnp.zeros((DN_HEADS,), F32)
    isdec_row = lanepad(jnp.tile(jnp.concatenate([jnp.ones((DN_HEADS,), F32), zeros4]), 2)[None, :])

    for l in range(depth):
        update_ctx = l < depth - 1
        mod = _mod(c_rows, w_mod[l], b_mod[l]).reshape(2 * SUBLANES, 6, 1, d)
        mx = lambda j: mod[:bsz, j]
        mc = lambda j: mod[bsz:bsz + 1, j]

        w_main = w_in[l][:, :nmain].astype(BF16)
        w_ab = lanepad(w_in[l][:, nmain:]).astype(BF16)
        alog_row = lanepad(jnp.concatenate([a_log[l][0], zeros4, a_log[l][1], zeros4])[None, :])
        dtb_row = lanepad(jnp.concatenate([dt_bias[l][0], zeros4, dt_bias[l][1], zeros4])[None, :])
        ip = lambda tok, sh, sc, rpg, slen: _inproj(tok, g_norm1[l], sc, sh, w_main, w_ab, conv_w[l], alog_row,
                                                     dtb_row, isdec_row, rpg, slen)
        f_x, qkv_x, z_x, gb_x = ip(xt, mx(0), mx(1), seq, seq)
        f_c, qkv_c, z_c, gb_c = ip(ct, mc(0), mc(1), bsz * clen, clen)

        s_zero = jnp.zeros((bsz, DN_HEADS, DN_HEAD_DIM, DN_HEAD_DIM), F32)
        ocf, ocb, s_f, s_b = _deltanet(qkv_c, gb_c, _decay_rows(gb_c, bsz, clen), s_zero, s_zero, bsz, clen)
        oxf, oxb, _, _ = _deltanet(qkv_x, gb_x, _decay_rows(gb_x, bsz, seq), s_f, s_b, bsz, seq)

        w_out_b = w_out[l].astype(BF16)
        w_r = lanepad(w_router[l])
        b_r = lanepad(b_router[l][None, :])
        four_x = _fourier(f_x, cp_x, sp_x, chan(np.cos(ang), seq), chan(np.sin(ang), seq), bsz, seq)
        cnt0 = jnp.zeros((1, LANES), F32)
        xn_x, h2_x, ti_x, tg_x, rk_x, cnt = _outproj(four_x, oxf, oxb, z_x, xt, g_out_norm[l], mx(2), mx(4), mx(3),
                                                      g_norm2[l], w_out_b, w_r, b_r, cnt0, seq)
        n_tok = bsz * seq
        if update_ctx:
            four_c = _fourier(f_c, cp_c, sp_c, chan(np.cos(ang), clen), chan(np.sin(ang), clen), bsz, clen)
            xn_c, h2_c, ti_c, tg_c, rk_c, cnt = _outproj(four_c, ocf, ocb, z_c, ct, g_out_norm[l], mc(2), mc(4),
                                                          mc(3), g_norm2[l], w_out_b, w_r, b_r, cnt, bsz * clen)
            n_tok += bsz * clen

        counts, pstart, padded, block_e, n_used, n_blocks = _slot_tables(cnt, n_tok)
        dest_x = _slot_of(ti_x, rk_x, pstart)
        streams = [(h2_x, dest_x)]
        if update_ctx:
            dest_c = _slot_of(ti_c, rk_c, pstart)
            streams.append((h2_c, dest_c))
        xs = _dispatch(streams, counts, pstart, padded, n_used, n_blocks)
        y_rt = _experts(xs, block_e, n_used, l, w_gate, b_gate, w_up, b_up, w_down, b_down)
        xt = _combine(xn_x, y_rt, dest_x, tg_x, mx(5), seq)
        if update_ctx:
            ct = _combine(xn_c, y_rt, dest_c, tg_c, mc(5), bsz * clen)

    return _final_norm(xt, g_final).reshape(bsz, seq, d)
```

```python
import functools
import math

import numpy as np
import jax
import jax.numpy as jnp
from jax import lax
from jax.experimental import pallas as pl
from jax.experimental.pallas import tpu as pltpu

FOURIER_GROUPS = 4
DN_HEADS = 4
DN_HEAD_DIM = 128
CONV_K = 5
N_EXPERTS = 32
TOP_K = 4
SWIGLU_LIMIT = 7.0
SWIGLU_ALPHA = 1.702
EPS = 1e-6

LANES = 128
SUBLANES = 8
V7X_VMEM_BYTES = 64 * 1024 * 1024
VMEM_LIMIT = V7X_VMEM_BYTES - 8 * 1024 * 1024
CHUNK = 64
GROUP = 4 * CHUNK
MOE_BLOCK = 256

F32 = jnp.float32
BF16 = jnp.bfloat16


def _cparams(*sem):
    return pltpu.CompilerParams(dimension_semantics=sem, vmem_limit_bytes=VMEM_LIMIT)


def _sigmoid(x):
    return 1.0 / (1.0 + jnp.exp(-x))


def _mod_kernel(c_ref, w_ref, b_ref, o_ref):
    c = c_ref[...]
    a = c * _sigmoid(c)
    o_ref[...] = jnp.dot(a, w_ref[...], preferred_element_type=F32) + b_ref[...]


def _mod(c_rows, w_mod, b_mod):
    r, d = c_rows.shape
    n = w_mod.shape[1]
    tn = n // 4
    return pl.pallas_call(
        _mod_kernel,
        grid=(n // tn,),
        in_specs=[pl.BlockSpec((r, d), lambda j: (0, 0)),
                  pl.BlockSpec((d, tn), lambda j: (0, j)),
                  pl.BlockSpec((1, tn), lambda j: (0, j))],
        out_specs=pl.BlockSpec((r, tn), lambda j: (0, j)),
        out_shape=jax.ShapeDtypeStruct((r, n), F32),
        compiler_params=_cparams("parallel"),
        name="mod",
    )(c_rows, w_mod, b_mod.reshape(1, n))


def _inproj_kernel(x_ref, xp_ref, xn_ref, g_ref, sc_ref, sh_ref, w_ref, wab_ref, cw_ref, alog_ref, dtb_ref,
                   isdec_ref, f_ref, qkv_ref, z_ref, gb_ref, *, fw, qw, seq_len):
    i = pl.program_id(0)
    tm = x_ref.shape[0]
    halo = SUBLANES

    def norm_mod(x):
        inv = lax.rsqrt(jnp.mean(x * x, axis=-1, keepdims=True) + EPS)
        return (x * inv * g_ref[...]) * (1.0 + sc_ref[...]) + sh_ref[...]

    h = norm_mod(x_ref[...])
    hb = h.astype(BF16)
    at_start = (i * tm) % seq_len == 0
    at_end = ((i + 1) * tm) % seq_len == 0
    hp = jnp.where(at_start, 0.0, norm_mod(xp_ref[...]))
    hn = jnp.where(at_end, 0.0, norm_mod(xn_ref[...]))
    hext = jnp.concatenate([hp, h, hn], axis=0).astype(BF16)
    f_ref[...] = jnp.dot(hb, w_ref[:, :fw], preferred_element_type=F32)
    n = tm + 2 * halo
    cwid = 2 * DN_HEAD_DIM
    for c in range(3 * qw // cwid):
        lo = c * cwid
        pre = jnp.dot(hext, w_ref[:, fw + lo:fw + lo + cwid], preferred_element_type=F32)
        acc = None
        for j in range(CONV_K):
            off = halo - CONV_K // 2 + j
            term = cw_ref[j:j + 1, lo:lo + cwid] * pltpu.roll(pre, n - off, 0)[:tm]
            acc = term if acc is None else acc + term
        y = acc * _sigmoid(acc)
        if lo < 2 * qw:
            scale = DN_HEAD_DIM ** -0.5 if lo < qw else 1.0
            segs = []
            for hh in range(cwid // DN_HEAD_DIM):
                ys = y[:, hh * DN_HEAD_DIM:(hh + 1) * DN_HEAD_DIM]
                segs.append(ys * (lax.rsqrt(jnp.sum(ys * ys, axis=-1, keepdims=True) + EPS) * scale))
            y = jnp.concatenate(segs, axis=1)
        qkv_ref[:, lo:lo + cwid] = y
    lo = fw + 3 * qw
    z_ref[...] = jnp.dot(hb, w_ref[:, lo:lo + qw], preferred_element_type=F32)
    ab = jnp.dot(hb, wab_ref[...], preferred_element_type=F32)
    pre = ab + dtb_ref[...]
    softplus = jnp.maximum(pre, 0.0) + jnp.log(1.0 + jnp.exp(-jnp.abs(pre)))
    dec = -jnp.exp(alog_ref[...]) * softplus
    gb = jnp.where(isdec_ref[...] > 0.0, dec, _sigmoid(ab))
    gb_ref[...] = gb[:, :gb_ref.shape[1]]


def _inproj(x, g1, sc, sh, w_main, w_ab, conv_w, alog_row, dtb_row, isdec_row, rows_per_group, seq_len):
    t, d = x.shape
    ng = sc.shape[0]
    fw = FOURIER_GROUPS * LANES
    qw = DN_HEADS * DN_HEAD_DIM
    nab = 4 * DN_HEADS
    tm = min(512, seq_len)
    hpt = tm // SUBLANES
    tpg = rows_per_group // tm
    grp = lambda i: (jnp.minimum(i // tpg, ng - 1), 0, 0)
    const = lambda i: (0, 0)
    return pl.pallas_call(
        functools.partial(_inproj_kernel, fw=fw, qw=qw, seq_len=seq_len),
        grid=(t // tm,),
        in_specs=[pl.BlockSpec((tm, d), lambda i: (i, 0)),
                  pl.BlockSpec((SUBLANES, d), lambda i: (jnp.maximum(i * hpt - 1, 0), 0)),
                  pl.BlockSpec((SUBLANES, d), lambda i: (jnp.minimum((i + 1) * hpt, t // SUBLANES - 1), 0)),
                  pl.BlockSpec((1, d), const),
                  pl.BlockSpec((None, 1, d), grp),
                  pl.BlockSpec((None, 1, d), grp),
                  pl.BlockSpec(w_main.shape, const),
                  pl.BlockSpec(w_ab.shape, const),
                  pl.BlockSpec(conv_w.shape, const),
                  pl.BlockSpec((1, LANES), const),
                  pl.BlockSpec((1, LANES), const),
                  pl.BlockSpec((1, LANES), const)],
        out_specs=[pl.BlockSpec((tm, fw), lambda i: (i, 0)),
                   pl.BlockSpec((tm, 3 * qw), lambda i: (i, 0)),
                   pl.BlockSpec((tm, qw), lambda i: (i, 0)),
                   pl.BlockSpec((tm, nab), lambda i: (i, 0))],
        out_shape=[jax.ShapeDtypeStruct((t, fw), F32),
                   jax.ShapeDtypeStruct((t, 3 * qw), F32),
                   jax.ShapeDtypeStruct((t, qw), F32),
                   jax.ShapeDtypeStruct((t, nab), F32)],
        compiler_params=_cparams("parallel"),
        name="inproj",
    )(x, x, x, g1.reshape(1, d), sc, sh, w_main, w_ab, conv_w, alog_row, dtb_row, isdec_row)


def _bdot(a, b):
    return jnp.dot(a.astype(BF16), b.astype(BF16), preferred_element_type=F32)


def _dot_nt(a, b):
    return lax.dot_general(a.astype(BF16), b.astype(BF16), (((1,), (1,)), ((), ())), preferred_element_type=F32)


def _dot_tn(a, b):
    return lax.dot_general(a, b, (((0,), (0,)), ((), ())), preferred_element_type=F32)


def _dn_masks(dirn):
    ii = lax.broadcasted_iota(jnp.int32, (GROUP, GROUP), 0)
    jj = lax.broadcasted_iota(jnp.int32, (GROUP, GROUP), 1)
    same = (ii // CHUNK) == (jj // CHUNK)
    if dirn == 0:
        return same, same & (ii >= jj), same & (ii > jj), same & (ii <= jj)
    return same, same & (ii <= jj), same & (ii < jj), same & (ii >= jj)


def _dn_groups(chains, states, q_s, k_s, v_s, gb_ref, gr_ref, o_refs):
    rng = range(len(chains))
    dh = DN_HEAD_DIM
    masks = {d: _dn_masks(d) for d in sorted({c[1] for c in chains})}
    same = [masks[c[1]][0] for c in chains]

    def pack(bd):
        return bd[0:CHUNK] + bd[CHUNK:2 * CHUNK] + bd[2 * CHUNK:3 * CHUNK] + bd[3 * CHUNK:]

    def unpack(pk, sm):
        return jnp.where(sm, jnp.concatenate([pk, pk, pk, pk], axis=0), 0.0)

    r0s, q, k, v, bcol, gam_col, tot_col, dec = [], [], [], [], [], [], [], []
    for hh, dirn, g in chains:
        r0 = pl.multiple_of(g * GROUP, GROUP)
        cols = slice(hh * dh, (hh + 1) * dh)
        r0s.append(r0)
        q.append(q_s[pl.ds(r0, GROUP), cols])
        k.append(k_s[pl.ds(r0, GROUP), cols])
        v.append(v_s[pl.ds(r0, GROUP), cols])
        cg = 2 * DN_HEADS * dirn + hh
        gcol = gb_ref[pl.ds(r0, GROUP), cg:cg + 1]
        bcol.append(gb_ref[pl.ds(r0, GROUP), cg + DN_HEADS:cg + DN_HEADS + 1])
        grow = gr_ref[hh, dirn, pl.ds(g, 1), :]
        sm, incl, _, incl_t = masks[dirn]
        gc = jnp.sum(jnp.where(incl, grow, 0.0), axis=1, keepdims=True)
        gr = jnp.sum(jnp.where(incl_t, gcol, 0.0), axis=0, keepdims=True)
        gam_col.append(gc)
        tot_col.append(jnp.sum(jnp.where(sm, grow, 0.0), axis=1, keepdims=True))
        dec.append(jnp.where(incl, jnp.exp(jnp.where(incl, gc - gr, 0.0)), 0.0))

    kk = [_dot_nt(k[i], k[i]) for i in rng]
    qk = [_dot_nt(q[i], k[i]) for i in rng]
    mk_bd = [jnp.where(masks[chains[i][1]][2], -(bcol[i] * kk[i] * dec[i]), 0.0) for i in rng]
    attn = [qk[i] * dec[i] for i in rng]

    pi = lax.broadcasted_iota(jnp.int32, (CHUNK, GROUP), 0)
    pj = lax.broadcasted_iota(jnp.int32, (CHUNK, GROUP), 1)
    eye_pk = jnp.where(pi == (pj % CHUNK), 1.0, 0.0)
    mk_pk = [pack(mk_bd[i]) for i in rng]
    p = [eye_pk + mk_pk[i] for i in rng]
    mk_b = [mk_bd[i].astype(BF16) for i in rng]
    for _ in range(5):
        mk_pk = [jnp.dot(mk_pk[i].astype(BF16), mk_b[i], preferred_element_type=F32) for i in rng]
        mk_b = [unpack(mk_pk[i], same[i]).astype(BF16) for i in rng]
        p = [p[i] + jnp.dot(p[i].astype(BF16), mk_b[i], preferred_element_type=F32) for i in rng]
    t_bd = [unpack(p[i], same[i]) for i in rng]

    eg = [jnp.exp(gam_col[i]) for i in rng]
    rhs = [jnp.concatenate([v[i] * bcol[i], k[i] * (bcol[i] * eg[i])], axis=1) for i in rng]
    uw = [_bdot(t_bd[i], rhs[i]) for i in rng]
    qg = [q[i] * eg[i] for i in rng]
    kend = [k[i] * jnp.exp(tot_col[i] - gam_col[i]) for i in rng]
    gend = [jnp.exp(tot_col[i]) for i in rng]

    zeros = jnp.zeros((CHUNK, dh), F32)
    for step in range(GROUP // CHUNK):
        cs = [step if chains[i][1] == 0 else GROUP // CHUNK - 1 - step for i in rng]
        lo = [c * CHUNK for c in cs]
        lhs = [jnp.concatenate([uw[i][lo[i]:lo[i] + CHUNK, dh:], qg[i][lo[i]:lo[i] + CHUNK]], axis=0)
               for i in rng]
        wq = [_bdot(lhs[i], states[i]) for i in rng]
        v_new = [uw[i][lo[i]:lo[i] + CHUNK, :dh] - wq[i][:CHUNK] for i in rng]
        states = [states[i] * gend[i][lo[i]:lo[i] + 1] + _dot_tn(kend[i][lo[i]:lo[i] + CHUNK], v_new[i])
                  for i in rng]
        for i in rng:
            v_full = jnp.concatenate([v_new[i] if cc == cs[i] else zeros for cc in range(GROUP // CHUNK)], axis=0)
            o_c = wq[i][CHUNK:] + _bdot(attn[i][lo[i]:lo[i] + CHUNK], v_full)
            hh = chains[i][0]
            o_refs[chains[i][1]][pl.ds(r0s[i] + lo[i], CHUNK), hh * dh:(hh + 1) * dh] = o_c
    return states


def _dn_kernel(q_s, k_s, v_s, gb_ref, gr_ref, s0f_ref, s0b_ref, of_ref, ob_ref, sf_ref, sb_ref, *, seq_len):
    ng = seq_len // GROUP
    nh = DN_HEADS

    def body(m, carry):
        chains = [(hh, d, m if d == 0 else ng - 1 - m) for hh in range(nh) for d in (0, 1)]
        return tuple(_dn_groups(chains, list(carry), q_s, k_s, v_s, gb_ref, gr_ref, (of_ref, ob_ref)))

    init = tuple(r[hh] for hh in range(nh) for r in (s0f_ref, s0b_ref))
    fin = lax.fori_loop(0, ng, body, init)
    for hh in range(nh):
        sf_ref[hh] = fin[2 * hh]
        sb_ref[hh] = fin[2 * hh + 1]


def _deltanet(qkv, gb, grow, s0f, s0b, bsz, seq_len):
    h, dh = DN_HEADS, DN_HEAD_DIM
    ng = seq_len // GROUP
    w = h * dh
    once = pl.Buffered(1)
    col = lambda j: pl.BlockSpec((seq_len, w), lambda b: (b, j), pipeline_mode=once)
    st = pl.BlockSpec((None, h, dh, dh), lambda b: (b, 0, 0, 0))
    return pl.pallas_call(
        functools.partial(_dn_kernel, seq_len=seq_len),
        grid=(bsz,),
        in_specs=[col(0), col(1), col(2),
                  pl.BlockSpec((seq_len, 4 * h), lambda b: (b, 0), pipeline_mode=once),
                  pl.BlockSpec((None, h, 2, ng, GROUP), lambda b: (b, 0, 0, 0, 0)),
                  st, st],
        out_specs=[col(0), col(0), st, st],
        out_shape=[jax.ShapeDtypeStruct((bsz * seq_len, w), F32),
                   jax.ShapeDtypeStruct((bsz * seq_len, w), F32),
                   jax.ShapeDtypeStruct((bsz, h, dh, dh), F32),
                   jax.ShapeDtypeStruct((bsz, h, dh, dh), F32)],
        compiler_params=_cparams("parallel"),
        name="deltanet",
    )(qkv, qkv, qkv, gb, grow, s0f, s0b)


def _dft_table_kernel(tac_ref, tas_ref, tbc_ref, tbs_ref, cp_ref, sp_ref, *, nt1):
    tbc = tbc_ref[...]
    tbs = tbs_ref[...]
    for t1 in range(nt1):
        ac = tac_ref[:, t1:t1 + 1]
        asn = tas_ref[:, t1:t1 + 1]
        cp_ref[:, t1 * LANES:(t1 + 1) * LANES] = (ac * tbc - asn * tbs).astype(BF16)
        sp_ref[:, t1 * LANES:(t1 + 1) * LANES] = (asn * tbc + ac * tbs).astype(BF16)


def _dft_tables(n):
    nt1 = n // LANES
    s = np.arange(n, dtype=np.int64)[:, None]
    t1 = np.arange(nt1, dtype=np.int64)[None, :]
    t0 = np.arange(LANES, dtype=np.int64)[None, :]
    ang_a = 2.0 * np.pi * ((s * t1) % nt1) / nt1
    ang_b = 2.0 * np.pi * ((s * t0) % n) / n
    pad = ((0, 0), (0, LANES - nt1))
    tac = jnp.asarray(np.pad(np.cos(ang_a), pad), F32)
    tas = jnp.asarray(np.pad(np.sin(ang_a), pad), F32)
    tbc = jnp.asarray(np.cos(ang_b), F32)
    tbs = jnp.asarray(np.sin(ang_b), F32)
    tr = min(n, 256)
    small = pl.BlockSpec((tr, LANES), lambda i: (i, 0))
    big = pl.BlockSpec((tr, n), lambda i: (i, 0))
    return pl.pallas_call(
        functools.partial(_dft_table_kernel, nt1=nt1),
        grid=(n // tr,),
        in_specs=[small, small, small, small],
        out_specs=[big, big],
        out_shape=[jax.ShapeDtypeStruct((n, n), BF16)] * 2,
        compiler_params=_cparams("parallel"),
        name="dft_tables",
    )(tac, tas, tbc, tbs)


MIRROR_HEAD = 2 * SUBLANES


def _fourier_kernel(f_ref, p1_ref, p2_ref, mid_ref, cp_ref, sp_ref, cc_ref, sc_ref, o_ref):
    kstep = pl.program_id(1)
    tk = f_ref.shape[0]
    seq_len = o_ref.shape[0]
    cc = cc_ref[...]
    sc = sc_ref[...]

    def chan(xb, tab):
        return jnp.concatenate([jnp.dot(xb[:, g * LANES:(g + 1) * LANES], tab, preferred_element_type=F32)
                                for g in range(FOURIER_GROUPS)], axis=1)

    xb = f_ref[...].astype(BF16)
    pb = jnp.concatenate([p1_ref[...], p2_ref[...]], axis=0).astype(BF16)
    ri = lax.broadcasted_iota(jnp.int32, (tk, tk + MIRROR_HEAD), 0)
    ci = lax.broadcasted_iota(jnp.int32, (tk, tk + MIRROR_HEAD), 1)
    sel = ((ri >= 1) & (ci == tk - ri)) | ((ri == 0) & (ci == tk) & (kstep > 0))
    xr = jnp.dot(jnp.where(sel, 1.0, 0.0).astype(BF16), pb, preferred_element_type=F32)
    xf = xb.astype(F32)
    zc = chan((xf + xr).astype(BF16), cc).astype(BF16)
    zs = chan((xf - xr).astype(BF16), sc).astype(BF16)
    part = (jnp.dot(cp_ref[...], zc, preferred_element_type=F32)
            - jnp.dot(sp_ref[...], zs, preferred_element_type=F32))

    @pl.when(kstep == 0)
    def _():
        zmid = chan(mid_ref[...].astype(BF16), cc)[0:1, :]
        s_idx = lax.broadcasted_iota(jnp.int32, (seq_len, 1), 0)
        o_ref[...] = part + jnp.where(s_idx % 2 == 0, 1.0, -1.0) * zmid

    @pl.when(kstep > 0)
    def _():
        o_ref[...] += part


def _fourier(f, cp, sp, cc, sc, bsz, seq_len):
    t, fw = f.shape
    tk = min(seq_len // 2, 512)
    nk = seq_len // tk
    hpb = tk // MIRROR_HEAD
    last_head = t // MIRROR_HEAD - 1
    return pl.pallas_call(
        _fourier_kernel,
        grid=(bsz, nk // 2),
        in_specs=[pl.BlockSpec((tk, fw), lambda b, k: (b * nk + k, 0)),
                  pl.BlockSpec((tk, fw), lambda b, k: (b * nk + nk - 1 - k, 0)),
                  pl.BlockSpec((MIRROR_HEAD, fw), lambda b, k: (jnp.minimum((b * nk + nk - k) * hpb, last_head), 0)),
                  pl.BlockSpec((MIRROR_HEAD, fw), lambda b, k: ((b * nk + nk // 2) * hpb, 0)),
                  pl.BlockSpec((seq_len, tk), lambda b, k: (0, k)),
                  pl.BlockSpec((seq_len, tk), lambda b, k: (0, k)),
                  pl.BlockSpec((LANES, LANES), lambda b, k: (0, 0)),
                  pl.BlockSpec((LANES, LANES), lambda b, k: (0, 0))],
        out_specs=pl.BlockSpec((seq_len, fw), lambda b, k: (b, 0), pipeline_mode=pl.Buffered(1)),
        out_shape=jax.ShapeDtypeStruct((bsz * seq_len, fw), F32),
        compiler_params=_cparams("parallel", "arbitrary"),
        name="fourier",
    )(f, f, f, f, cp, sp, cc, sc)


def _outproj_kernel(four_ref, of_ref, ob_ref, z_ref, x_ref, gout_ref, gt1_ref, sc2_ref, sh2_ref, g2_ref,
                    wout_ref, wr_ref, br_ref, cnt0_ref, xn_ref, h2_ref, ti_ref, tg_ref, rk_ref, cnt_ref,
                    carry, *, fw):
    @pl.when(pl.program_id(0) == 0)
    def _():
        carry[...] = cnt0_ref[...]

    o = of_ref[...] + ob_ref[...]
    z = z_ref[...]
    gout = gout_ref[...]
    parts = []
    for hh in range(DN_HEADS):
        sl = slice(hh * DN_HEAD_DIM, (hh + 1) * DN_HEAD_DIM)
        oh = o[:, sl]
        zh = z[:, sl]
        oh = oh * lax.rsqrt(jnp.mean(oh * oh, axis=-1, keepdims=True) + EPS) * gout
        parts.append(oh * (zh * _sigmoid(zh)))
    gated = jnp.concatenate(parts, axis=1).astype(BF16)
    mix = (jnp.dot(four_ref[...].astype(BF16), wout_ref[:fw, :], preferred_element_type=F32)
           + jnp.dot(gated, wout_ref[fw:, :], preferred_element_type=F32))
    xn = x_ref[...] + gt1_ref[...] * mix
    xn_ref[...] = xn
    inv = lax.rsqrt(jnp.mean(xn * xn, axis=-1, keepdims=True) + EPS)
    h2 = (xn * inv * g2_ref[...]) * (1.0 + sc2_ref[...]) + sh2_ref[...]
    tm = h2.shape[0]
    for s in range(h2.shape[1] // LANES):
        h2_ref[pl.ds(s, tm, stride=SUBLANES), :] = h2[:, s * LANES:(s + 1) * LANES]
    wr = wr_ref[...]
    wr_hi = wr.astype(BF16)
    wr_lo = (wr - wr_hi.astype(F32)).astype(BF16)
    h2_hi = h2.astype(BF16)
    h2_lo = (h2 - h2_hi.astype(F32)).astype(BF16)
    logits = (jnp.dot(h2_hi, wr_hi, preferred_element_type=F32) + jnp.dot(h2_hi, wr_lo, preferred_element_type=F32)
              + jnp.dot(h2_lo, wr_hi, preferred_element_type=F32)) + br_ref[...]
    lane = lax.broadcasted_iota(jnp.int32, logits.shape, 1)
    neg = jnp.float32(-jnp.inf)
    logits = jnp.where(lane < N_EXPERTS, logits, neg)
    vals, idxs = [], []
    for _ in range(TOP_K):
        m = jnp.max(logits, axis=-1, keepdims=True)
        idx = jnp.min(jnp.where(logits == m, lane, LANES), axis=-1, keepdims=True)
        vals.append(m)
        idxs.append(idx)
        logits = jnp.where(lane == idx, neg, logits)
    ex = [jnp.exp(vv - vals[0]) for vv in vals]
    den = ex[0] + ex[1] + ex[2] + ex[3]
    onehot = jnp.zeros(logits.shape, F32)
    for j in range(TOP_K):
        onehot = onehot + jnp.where(lane == idxs[j], 1.0, 0.0)
    ri = lax.broadcasted_iota(jnp.int32, (tm, tm), 0)
    ci = lax.broadcasted_iota(jnp.int32, (tm, tm), 1)
    tri = jnp.where(ri > ci, 1.0, 0.0).astype(BF16)
    base = jnp.dot(tri, onehot.astype(BF16), preferred_element_type=F32) + carry[...]
    carry[...] = carry[...] + jnp.sum(onehot, axis=0, keepdims=True)
    cnt_ref[...] = carry[...]
    kl = lax.broadcasted_iota(jnp.int32, (tm, TOP_K), 1)
    ti = jnp.zeros((tm, TOP_K), jnp.int32)
    tg = jnp.zeros((tm, TOP_K), F32)
    rk = jnp.zeros((tm, TOP_K), jnp.int32)
    for j in range(TOP_K):
        rank_j = jnp.sum(jnp.where(lane == idxs[j], base, 0.0), axis=-1, keepdims=True).astype(jnp.int32)
        ti = jnp.where(kl == j, idxs[j], ti)
        tg = jnp.where(kl == j, ex[j] / den, tg)
        rk = jnp.where(kl == j, rank_j, rk)
    ti_ref[...] = ti
    tg_ref[...] = tg
    rk_ref[...] = rk


def _outproj(four, o_f, o_b, z, x, g_out, gt1, sc2, sh2, g2, w_out, w_r, b_r, cnt0, rows_per_group, tm=256):
    t, d = x.shape
    fw = four.shape[1]
    ng = gt1.shape[0]
    tpg = rows_per_group // tm
    grp = lambda i: (jnp.minimum(i // tpg, ng - 1), 0, 0)
    const = lambda i: (0, 0)
    row = lambda w: pl.BlockSpec((tm, w), lambda i: (i, 0))
    gspec = pl.BlockSpec((None, 1, d), grp)
    return pl.pallas_call(
        functools.partial(_outproj_kernel, fw=fw),
        grid=(t // tm,),
        in_specs=[row(fw), row(z.shape[1]), row(z.shape[1]), row(z.shape[1]), row(d),
                  pl.BlockSpec((1, DN_HEAD_DIM), const), gspec, gspec, gspec,
                  pl.BlockSpec((1, d), const),
                  pl.BlockSpec(w_out.shape, const),
                  pl.BlockSpec(w_r.shape, const),
                  pl.BlockSpec((1, LANES), const),
                  pl.BlockSpec((1, LANES), const)],
        out_specs=[row(d),
                   pl.BlockSpec((tm * SUBLANES, LANES), lambda i: (i, 0)),
                   row(TOP_K), row(TOP_K), row(TOP_K),
                   pl.BlockSpec((1, LANES), const)],
        out_shape=[jax.ShapeDtypeStruct((t, d), F32),
                   jax.ShapeDtypeStruct((t * d // LANES, LANES), F32),
                   jax.ShapeDtypeStruct((t, TOP_K), jnp.int32),
                   jax.ShapeDtypeStruct((t, TOP_K), F32),
                   jax.ShapeDtypeStruct((t, TOP_K), jnp.int32),
                   jax.ShapeDtypeStruct((1, LANES), F32)],
        scratch_shapes=[pltpu.VMEM((1, LANES), F32)],
        compiler_params=_cparams("arbitrary"),
        name="outproj",
    )(four, o_f, o_b, z, x, g_out.reshape(1, DN_HEAD_DIM), gt1, sc2, sh2, g2.reshape(1, d), w_out, w_r, b_r, cnt0)


TOK_UNROLL = 4


def _row_copy(src, src_row, dst, dst_row, sem):
    return pltpu.make_async_copy(
        src.at[pl.ds(pl.multiple_of(src_row * SUBLANES, SUBLANES), SUBLANES), :],
        dst.at[pl.ds(pl.multiple_of(dst_row * SUBLANES, SUBLANES), SUBLANES), :], sem)


def _dispatch_kernel(cnt_ref, ps_ref, pd_ref, nu_ref, *refs, tm, steps, n_blocks):
    ns = len(steps)
    dest_refs, h_refs = refs[0:2 * ns:2], refs[1:2 * ns:2]
    xs_hbm, zero_s, sem, sem_pad = refs[2 * ns:]
    i = pl.program_id(0)
    blk_rows = MOE_BLOCK * SUBLANES

    def tail_copy(b):
        return pltpu.make_async_copy(zero_s, xs_hbm.at[pl.ds(pl.multiple_of(b * blk_rows, blk_rows), blk_rows), :],
                                     sem_pad)

    @pl.when(i == 0)
    def _():
        zero_s[...] = jnp.zeros(zero_s.shape, F32)

        def per_expert(e, tot):
            def one(r, c):
                _row_copy(zero_s, 0, xs_hbm, ps_ref[e] + r, sem_pad).start()
                return c

            lax.fori_loop(cnt_ref[e], pd_ref[e], one, 0)
            return tot + pd_ref[e] - cnt_ref[e]

        n_pad = lax.fori_loop(0, N_EXPERTS, per_expert, 0)

        def start_tail(b, c):
            tail_copy(b).start()
            return c

        lax.fori_loop(nu_ref[0], n_blocks, start_tail, 0)

        def wait_row(r, c):
            _row_copy(zero_s, 0, xs_hbm, 0, sem_pad).wait()
            return c

        lax.fori_loop(0, n_pad, wait_row, 0)

        def wait_tail(b, c):
            tail_copy(b).wait()
            return c

        lax.fori_loop(nu_ref[0], n_blocks, wait_tail, 0)

    first = 0
    for dest_ref, h_ref, nst in zip(dest_refs, h_refs, steps):
        @pl.when((i >= first) & (i < first + nst))
        def _(dest_ref=dest_ref, h_ref=h_ref):
            def issue(blk, carry):
                for tt in range(TOK_UNROLL):
                    t = blk * TOK_UNROLL + tt
                    for j in range(TOP_K):
                        _row_copy(h_ref, t, xs_hbm, dest_ref[0, 0, t * TOP_K + j], sem).start(priority=j % 2)
                return carry

            lax.fori_loop(0, tm // TOK_UNROLL, issue, 0)
            for _ in range(TOP_K):
                pltpu.make_async_copy(h_ref, xs_hbm.at[pl.ds(0, tm * SUBLANES), :], sem).wait()

        first += nst


def _dispatch(streams, counts, pstart, padded, n_used, n_blocks, tm=256):
    seg = streams[0][0].shape[0] // streams[0][1].shape[0]
    steps = [dest.shape[0] // tm for _, dest in streams]
    in_specs, args, first = [], [], 0
    for (h_rt, dest), nst in zip(streams, steps):
        idx = lambda i, *_, first=first, nst=nst: (jnp.clip(i - first, 0, nst - 1), 0)
        idx3 = lambda i, *_, idx=idx: idx(i) + (0,)
        in_specs += [pl.BlockSpec((1, 1, tm * TOP_K), idx3, memory_space=pltpu.SMEM),
                     pl.BlockSpec((tm * seg, LANES), idx)]
        args += [dest.reshape(nst, 1, tm * TOP_K), h_rt]
        first += nst
    grid_spec = pltpu.PrefetchScalarGridSpec(
        num_scalar_prefetch=4,
        grid=(first,),
        in_specs=in_specs,
        out_specs=pl.BlockSpec(memory_space=pltpu.HBM),
        scratch_shapes=[pltpu.VMEM((MOE_BLOCK * seg, LANES), F32),
                        pltpu.SemaphoreType.DMA(()), pltpu.SemaphoreType.DMA(())],
    )
    return pl.pallas_call(
        functools.partial(_dispatch_kernel, tm=tm, steps=tuple(steps), n_blocks=n_blocks),
        grid_spec=grid_spec,
        out_shape=jax.ShapeDtypeStruct((n_blocks * MOE_BLOCK * seg, LANES), F32),
        compiler_params=_cparams("arbitrary"),
        name="dispatch",
    )(counts, pstart, padded, n_used, *args)


def _expert_kernel(be_ref, nu_ref, xs_ref, wg_ref, bg_ref, wu_ref, bu_ref, wd_ref, bd_ref, y_ref,
                   x2d, wg_b, wu_b, wd_b):
    i = pl.program_id(0)
    rows = MOE_BLOCK
    nseg = x2d.shape[1] // LANES

    @pl.when(i < nu_ref[0])
    def _():
        @pl.when((i == 0) | (be_ref[i] != be_ref[jnp.maximum(i - 1, 0)]))
        def _():
            wg_b[...] = wg_ref[...].astype(BF16)
            wu_b[...] = wu_ref[...].astype(BF16)
            wd_b[...] = wd_ref[...].astype(BF16)

        for s in range(nseg):
            x2d[:, s * LANES:(s + 1) * LANES] = xs_ref[pl.ds(s, rows, stride=SUBLANES), :].astype(BF16)
        x = x2d[...]
        a = jnp.minimum(jnp.dot(x, wg_b[...], preferred_element_type=F32) + bg_ref[...], SWIGLU_LIMIT)
        u = jnp.clip(jnp.dot(x, wu_b[...], preferred_element_type=F32) + bu_ref[...], -SWIGLU_LIMIT, SWIGLU_LIMIT)
        act = (a * _sigmoid(SWIGLU_ALPHA * a) * (u + 1.0)).astype(BF16)
        y = jnp.dot(act, wd_b[...], preferred_element_type=F32) + bd_ref[...]
        for s in range(nseg):
            y_ref[pl.ds(s, rows, stride=SUBLANES), :] = y[:, s * LANES:(s + 1) * LANES]

    @pl.when(i >= nu_ref[0])
    def _():
        y_ref[...] = jnp.zeros(y_ref.shape, F32)


def _experts(xs, block_e, n_used, layer, w_gate, b_gate, w_up, b_up, w_down, b_down):
    n_blocks = block_e.shape[0]
    depth, ne, d, ff = w_gate.shape
    seg = d // LANES
    blk = lambda i, nu: jnp.minimum(i, jnp.maximum(nu[0] - 1, 0))
    wspec = lambda shp: pl.BlockSpec((None, None) + shp, lambda i, be, nu: (layer, be[blk(i, nu)], 0, 0))
    grid_spec = pltpu.PrefetchScalarGridSpec(
        num_scalar_prefetch=2,
        grid=(n_blocks,),
        in_specs=[pl.BlockSpec((MOE_BLOCK * seg, LANES), lambda i, be, nu: (blk(i, nu), 0)),
                  wspec((d, ff)), wspec((1, ff)), wspec((d, ff)), wspec((1, ff)), wspec((ff, d)), wspec((1, d))],
        out_specs=pl.BlockSpec((MOE_BLOCK * seg, LANES), lambda i, be, nu: (i, 0)),
        scratch_shapes=[pltpu.VMEM((MOE_BLOCK, d), BF16), pltpu.VMEM((d, ff), BF16), pltpu.VMEM((d, ff), BF16),
                        pltpu.VMEM((ff, d), BF16)],
    )
    return pl.pallas_call(
        _expert_kernel,
        grid_spec=grid_spec,
        out_shape=jax.ShapeDtypeStruct((n_blocks * MOE_BLOCK * seg, LANES), F32),
        compiler_params=_cparams("arbitrary"),
        name="experts",
    )(block_e, n_used, xs, w_gate, b_gate.reshape(depth, ne, 1, ff), w_up, b_up.reshape(depth, ne, 1, ff),
      w_down, b_down.reshape(depth, ne, 1, d))


def _combine_kernel(dest_ref, dnext_ref, y_hbm, x_ref, tg_ref, gt_ref, gf_ref, o_ref, buf0, buf1, sem0, sem1, *,
                    tm, final_norm):
    n = tm * TOP_K
    i = pl.program_id(0)
    last = pl.num_programs(0) - 1

    def gather(d_ref, buf, sem):
        def issue(blk, carry):
            for tt in range(TOK_UNROLL):
                t = blk * TOK_UNROLL + tt
                for j in range(TOP_K):
                    _row_copy(y_hbm, d_ref[0, 0, t * TOP_K + j], buf, j * tm + t, sem).start(priority=j % 2)
            return carry

        lax.fori_loop(0, tm // TOK_UNROLL, issue, 0)

    def reduce(buf, sem):
        pltpu.make_async_copy(y_hbm.at[pl.ds(0, n * SUBLANES), :], buf, sem).wait()
        gt = gt_ref[...]
        tg = tg_ref[...]
        d = o_ref.shape[1]
        ssq = jnp.zeros((tm, 1), F32)
        for s in range(d // LANES):
            acc = tg[:, 0:1] * buf[pl.ds(s, tm, stride=SUBLANES), :]
            for j in range(1, TOP_K):
                acc = acc + tg[:, j:j + 1] * buf[pl.ds(j * tm * SUBLANES + s, tm, stride=SUBLANES), :]
            sl = slice(s * LANES, (s + 1) * LANES)
            o = x_ref[:, sl] + gt[:, sl] * acc
            o_ref[:, sl] = o
            if final_norm:
                ssq = ssq + jnp.sum(o * o, axis=-1, keepdims=True)
        if final_norm:
            inv = lax.rsqrt(ssq / d + EPS)
            o_ref[...] = o_ref[...] * inv * gf_ref[...]

    @pl.when(i == 0)
    def _():
        gather(dest_ref, buf0, sem0)

    for par, (cur, nxt) in enumerate((((buf0, sem0), (buf1, sem1)), ((buf1, sem1), (buf0, sem0)))):
        @pl.when(i % 2 == par)
        def _(cur=cur, nxt=nxt):
            @pl.when(i < last)
            def _():
                gather(dnext_ref, *nxt)

            reduce(*cur)


def _combine(x, y_rt, dest, top_g, gt2, g_final, final_norm, rows_per_group, tm=128):
    t, d = x.shape
    ng = gt2.shape[0]
    nt = t // tm
    tpg = rows_per_group // tm
    dest3 = dest.reshape(nt, 1, tm * TOP_K)
    buf = pltpu.VMEM((tm * TOP_K * SUBLANES, LANES), F32)
    return pl.pallas_call(
        functools.partial(_combine_kernel, tm=tm, final_norm=final_norm),
        grid=(nt,),
        in_specs=[pl.BlockSpec((1, 1, tm * TOP_K), lambda i: (i, 0, 0), memory_space=pltpu.SMEM),
                  pl.BlockSpec((1, 1, tm * TOP_K), lambda i: (jnp.minimum(i + 1, nt - 1), 0, 0),
                               memory_space=pltpu.SMEM),
                  pl.BlockSpec(memory_space=pltpu.HBM),
                  pl.BlockSpec((tm, d), lambda i: (i, 0)),
                  pl.BlockSpec((tm, TOP_K), lambda i: (i, 0)),
                  pl.BlockSpec((None, 1, d), lambda i: (jnp.minimum(i // tpg, ng - 1), 0, 0)),
                  pl.BlockSpec((1, d), lambda i: (0, 0))],
        out_specs=pl.BlockSpec((tm, d), lambda i: (i, 0)),
        out_shape=jax.ShapeDtypeStruct((t, d), F32),
        scratch_shapes=[buf, buf, pltpu.SemaphoreType.DMA(()), pltpu.SemaphoreType.DMA(())],
        compiler_params=_cparams("arbitrary"),
        name="combine",
    )(dest3, dest3, y_rt, x, top_g, gt2, g_final.reshape(1, d))


def _slot_tables(cnt_row, n_tok):
    n_blocks = -(-n_tok * TOP_K // MOE_BLOCK) + N_EXPERTS
    counts = cnt_row[0, :N_EXPERTS].astype(jnp.int32)
    padded = (counts + MOE_BLOCK - 1) // MOE_BLOCK * MOE_BLOCK
    padded_end = jnp.cumsum(padded)
    pstart = padded_end - padded
    blk_start = jnp.arange(n_blocks, dtype=jnp.int32) * MOE_BLOCK
    block_e = jnp.minimum(jnp.sum(padded_end[None, :] <= blk_start[:, None], axis=1), N_EXPERTS - 1).astype(jnp.int32)
    n_used = (padded_end[-1:] // MOE_BLOCK).astype(jnp.int32)
    return counts, pstart.astype(jnp.int32), padded.astype(jnp.int32), block_e, n_used, n_blocks


def _slot_of(top_i, rank, pstart):
    e = jnp.arange(N_EXPERTS, dtype=jnp.int32)
    return rank + jnp.sum(jnp.where(top_i[..., None] == e, pstart, 0), axis=-1)


def _decay_rows(gb, bsz, seq_len):
    g5 = gb.reshape(bsz, seq_len, 2, 2, DN_HEADS)
    return g5[:, :, :, 0, :].transpose(0, 3, 2, 1).reshape(bsz, DN_HEADS, 2, seq_len // GROUP, GROUP)


def kernel(x, c, ctx, c_ctx, w_mod, b_mod, g_norm1, w_in, conv_w, a_log, dt_bias, g_out_norm, w_out, g_norm2,
           w_router, b_router, w_gate, b_gate, w_up, b_up, w_down, b_down, g_final):
    bsz, seq, d = x.shape
    clen = ctx.shape[1]
    depth = w_mod.shape[0]
    fw = FOURIER_GROUPS * LANES
    nmain = fw + 4 * DN_HEADS * DN_HEAD_DIM
    nab = 4 * DN_HEADS

    xt = x.reshape(bsz * seq, d)
    ct = ctx.reshape(bsz * clen, d)
    c_rows = jnp.concatenate([c, c_ctx[None, :], jnp.zeros((2 * SUBLANES - bsz - 1, d), F32)], axis=0)

    cp_x, sp_x = _dft_tables(seq)
    cp_c, sp_c = _dft_tables(clen)
    ch = np.arange(LANES, dtype=np.int64)
    ang = 2.0 * np.pi * ((ch[:, None] * ch[None, :]) % LANES) / LANES

    def chan(tab, n):
        return jnp.asarray(tab / math.sqrt(n * LANES), F32).astype(BF16)

    lanepad = lambda v: jnp.pad(v, ((0, 0), (0, LANES - v.shape[1])))
    zeros4 = jnp.zeros((DN_HEADS,), F32)
    isdec_row = lanepad(jnp.tile(jnp.concatenate([jnp.ones((DN_HEADS,), F32), zeros4]), 2)[None, :])

    for l in range(depth):
        update_ctx = l < depth - 1
        mod = _mod(c_rows, w_mod[l], b_mod[l]).reshape(2 * SUBLANES, 6, 1, d)
        mx = lambda j: mod[:bsz, j]
        mc = lambda j: mod[bsz:bsz + 1, j]

        w_main = w_in[l][:, :nmain].astype(BF16)
        w_ab = lanepad(w_in[l][:, nmain:]).astype(BF16)
        alog_row = lanepad(jnp.concatenate([a_log[l][0], zeros4, a_log[l][1], zeros4])[None, :])
        dtb_row = lanepad(jnp.concatenate([dt_bias[l][0], zeros4, dt_bias[l][1], zeros4])[None, :])
        ip = lambda tok, sh, sc, rpg, slen: _inproj(tok, g_norm1[l], sc, sh, w_main, w_ab, conv_w[l], alog_row,
                                                     dtb_row, isdec_row, rpg, slen)
        f_x, qkv_x, z_x, gb_x = ip(xt, mx(0), mx(1), seq, seq)
        f_c, qkv_c, z_c, gb_c = ip(ct, mc(0), mc(1), bsz * clen, clen)

        s_zero = jnp.zeros((bsz, DN_HEADS, DN_HEAD_DIM, DN_HEAD_DIM), F32)
        ocf, ocb, s_f, s_b = _deltanet(qkv_c, gb_c, _decay_rows(gb_c, bsz, clen), s_zero, s_zero, bsz, clen)
        oxf, oxb, _, _ = _deltanet(qkv_x, gb_x, _decay_rows(gb_x, bsz, seq), s_f, s_b, bsz, seq)

        w_out_b = w_out[l].astype(BF16)
        w_r = lanepad(w_router[l])
        b_r = lanepad(b_router[l][None, :])
        four_x = _fourier(f_x, cp_x, sp_x, chan(np.cos(ang), seq), chan(np.sin(ang), seq), bsz, seq)
        cnt0 = jnp.zeros((1, LANES), F32)
        xn_x, h2_x, ti_x, tg_x, rk_x, cnt = _outproj(four_x, oxf, oxb, z_x, xt, g_out_norm[l], mx(2), mx(4), mx(3),
                                                      g_norm2[l], w_out_b, w_r, b_r, cnt0, seq)
        n_tok = bsz * seq
        if update_ctx:
            four_c = _fourier(f_c, cp_c, sp_c, chan(np.cos(ang), clen), chan(np.sin(ang), clen), bsz, clen)
            xn_c, h2_c, ti_c, tg_c, rk_c, cnt = _outproj(four_c, ocf, ocb, z_c, ct, g_out_norm[l], mc(2), mc(4),
                                                          mc(3), g_norm2[l], w_out_b, w_r, b_r, cnt, bsz * clen)
            n_tok += bsz * clen

        counts, pstart, padded, block_e, n_used, n_blocks = _slot_tables(cnt, n_tok)
        dest_x = _slot_of(ti_x, rk_x, pstart)
        streams = [(h2_x, dest_x)]
        if update_ctx:
            dest_c = _slot_of(ti_c, rk_c, pstart)
            streams.append((h2_c, dest_c))
        xs = _dispatch(streams, counts, pstart, padded, n_used, n_blocks)
        y_rt = _experts(xs, block_e, n_used, l, w_gate, b_gate, w_up, b_up, w_down, b_down)
        xt = _combine(xn_x, y_rt, dest_x, tg_x, mx(5), g_final, l == depth - 1, seq)
        if update_ctx:
            ct = _combine(xn_c, y_rt, dest_c, tg_c, mc(5), g_final, False, bsz * clen)

    return xt.reshape(bsz, seq, d)
```

```python
import functools
import math

import numpy as np
import jax
import jax.numpy as jnp
from jax import lax
from jax.experimental import pallas as pl
from jax.experimental.pallas import tpu as pltpu

FOURIER_GROUPS = 4
DN_HEADS = 4
DN_HEAD_DIM = 128
CONV_K = 5
N_EXPERTS = 32
TOP_K = 4
SWIGLU_LIMIT = 7.0
SWIGLU_ALPHA = 1.702
EPS = 1e-6

LANES = 128
SUBLANES = 8
V7X_VMEM_BYTES = 64 * 1024 * 1024
VMEM_LIMIT = V7X_VMEM_BYTES - 8 * 1024 * 1024
CHUNK = 64
GROUP = 4 * CHUNK
MOE_BLOCK = 256

F32 = jnp.float32
BF16 = jnp.bfloat16


def _cparams(*sem):
    return pltpu.CompilerParams(dimension_semantics=sem, vmem_limit_bytes=VMEM_LIMIT)


def _sigmoid(x):
    return 1.0 / (1.0 + jnp.exp(-x))


def _mod_kernel(c_ref, w_ref, b_ref, o_ref):
    c = c_ref[...]
    a = c * _sigmoid(c)
    o_ref[...] = jnp.dot(a, w_ref[...], preferred_element_type=F32) + b_ref[...]


def _mod(c_rows, w_mod, b_mod):
    r, d = c_rows.shape
    n = w_mod.shape[1]
    tn = n // 4
    return pl.pallas_call(
        _mod_kernel,
        grid=(n // tn,),
        in_specs=[pl.BlockSpec((r, d), lambda j: (0, 0)),
                  pl.BlockSpec((d, tn), lambda j: (0, j)),
                  pl.BlockSpec((1, tn), lambda j: (0, j))],
        out_specs=pl.BlockSpec((r, tn), lambda j: (0, j)),
        out_shape=jax.ShapeDtypeStruct((r, n), F32),
        compiler_params=_cparams("parallel"),
        name="mod",
    )(c_rows, w_mod, b_mod.reshape(1, n))


def _inproj_kernel(x_ref, xp_ref, xn_ref, g_ref, sc_ref, sh_ref, w_ref, wab_ref, cw_ref, alog_ref, dtb_ref,
                   isdec_ref, f_ref, qkv_ref, z_ref, gb_ref, *, fw, qw, seq_len):
    i = pl.program_id(0)
    tm = x_ref.shape[0]
    halo = SUBLANES

    def norm_mod(x):
        inv = lax.rsqrt(jnp.mean(x * x, axis=-1, keepdims=True) + EPS)
        return (x * inv * g_ref[...]) * (1.0 + sc_ref[...]) + sh_ref[...]

    h = norm_mod(x_ref[...])
    hb = h.astype(BF16)
    at_start = (i * tm) % seq_len == 0
    at_end = ((i + 1) * tm) % seq_len == 0
    hp = jnp.where(at_start, 0.0, norm_mod(xp_ref[...]))
    hn = jnp.where(at_end, 0.0, norm_mod(xn_ref[...]))
    hext = jnp.concatenate([hp, h, hn], axis=0).astype(BF16)
    f_ref[...] = jnp.dot(hb, w_ref[:, :fw], preferred_element_type=F32)
    n = tm + 2 * halo
    cwid = 2 * DN_HEAD_DIM
    for c in range(3 * qw // cwid):
        lo = c * cwid
        pre = jnp.dot(hext, w_ref[:, fw + lo:fw + lo + cwid], preferred_element_type=F32)
        acc = None
        for j in range(CONV_K):
            off = halo - CONV_K // 2 + j
            term = cw_ref[j:j + 1, lo:lo + cwid] * pltpu.roll(pre, n - off, 0)[:tm]
            acc = term if acc is None else acc + term
        y = acc * _sigmoid(acc)
        if lo < 2 * qw:
            scale = DN_HEAD_DIM ** -0.5 if lo < qw else 1.0
            segs = []
            for hh in range(cwid // DN_HEAD_DIM):
                ys = y[:, hh * DN_HEAD_DIM:(hh + 1) * DN_HEAD_DIM]
                segs.append(ys * (lax.rsqrt(jnp.sum(ys * ys, axis=-1, keepdims=True) + EPS) * scale))
            y = jnp.concatenate(segs, axis=1)
        qkv_ref[:, lo:lo + cwid] = y
    lo = fw + 3 * qw
    z_ref[...] = jnp.dot(hb, w_ref[:, lo:lo + qw], preferred_element_type=F32)
    ab = jnp.dot(hb, wab_ref[...], preferred_element_type=F32)
    pre = ab + dtb_ref[...]
    softplus = jnp.maximum(pre, 0.0) + jnp.log(1.0 + jnp.exp(-jnp.abs(pre)))
    dec = -jnp.exp(alog_ref[...]) * softplus
    gb = jnp.where(isdec_ref[...] > 0.0, dec, _sigmoid(ab))
    gb_ref[...] = gb[:, :gb_ref.shape[1]]


def _inproj(x, g1, sc, sh, w_main, w_ab, conv_w, alog_row, dtb_row, isdec_row, rows_per_group, seq_len):
    t, d = x.shape
    ng = sc.shape[0]
    fw = FOURIER_GROUPS * LANES
    qw = DN_HEADS * DN_HEAD_DIM
    nab = 4 * DN_HEADS
    tm = min(512, seq_len)
    hpt = tm // SUBLANES
    tpg = rows_per_group // tm
    grp = lambda i: (jnp.minimum(i // tpg, ng - 1), 0, 0)
    const = lambda i: (0, 0)
    return pl.pallas_call(
        functools.partial(_inproj_kernel, fw=fw, qw=qw, seq_len=seq_len),
        grid=(t // tm,),
        in_specs=[pl.BlockSpec((tm, d), lambda i: (i, 0)),
                  pl.BlockSpec((SUBLANES, d), lambda i: (jnp.maximum(i * hpt - 1, 0), 0)),
                  pl.BlockSpec((SUBLANES, d), lambda i: (jnp.minimum((i + 1) * hpt, t // SUBLANES - 1), 0)),
                  pl.BlockSpec((1, d), const),
                  pl.BlockSpec((None, 1, d), grp),
                  pl.BlockSpec((None, 1, d), grp),
                  pl.BlockSpec(w_main.shape, const),
                  pl.BlockSpec(w_ab.shape, const),
                  pl.BlockSpec(conv_w.shape, const),
                  pl.BlockSpec((1, LANES), const),
                  pl.BlockSpec((1, LANES), const),
                  pl.BlockSpec((1, LANES), const)],
        out_specs=[pl.BlockSpec((tm, fw), lambda i: (i, 0)),
                   pl.BlockSpec((tm, 3 * qw), lambda i: (i, 0)),
                   pl.BlockSpec((tm, qw), lambda i: (i, 0)),
                   pl.BlockSpec((tm, nab), lambda i: (i, 0))],
        out_shape=[jax.ShapeDtypeStruct((t, fw), F32),
                   jax.ShapeDtypeStruct((t, 3 * qw), F32),
                   jax.ShapeDtypeStruct((t, qw), F32),
                   jax.ShapeDtypeStruct((t, nab), F32)],
        compiler_params=_cparams("parallel"),
        name="inproj",
    )(x, x, x, g1.reshape(1, d), sc, sh, w_main, w_ab, conv_w, alog_row, dtb_row, isdec_row)


def _bdot(a, b):
    return jnp.dot(a.astype(BF16), b.astype(BF16), preferred_element_type=F32)


def _dot_nt(a, b):
    return lax.dot_general(a.astype(BF16), b.astype(BF16), (((1,), (1,)), ((), ())), preferred_element_type=F32)


def _dot_tn(a, b):
    return lax.dot_general(a, b, (((0,), (0,)), ((), ())), preferred_element_type=F32)


def _dn_masks(dirn):
    ii = lax.broadcasted_iota(jnp.int32, (GROUP, GROUP), 0)
    jj = lax.broadcasted_iota(jnp.int32, (GROUP, GROUP), 1)
    same = (ii // CHUNK) == (jj // CHUNK)
    if dirn == 0:
        return same, same & (ii >= jj), same & (ii > jj), same & (ii <= jj)
    return same, same & (ii <= jj), same & (ii < jj), same & (ii >= jj)


def _dn_groups(chains, states, q_s, k_s, v_s, gb_ref, gr_ref, o_refs):
    rng = range(len(chains))
    dh = DN_HEAD_DIM
    masks = {d: _dn_masks(d) for d in sorted({c[1] for c in chains})}
    same = [masks[c[1]][0] for c in chains]

    def pack(bd):
        return bd[0:CHUNK] + bd[CHUNK:2 * CHUNK] + bd[2 * CHUNK:3 * CHUNK] + bd[3 * CHUNK:]

    def unpack(pk, sm):
        return jnp.where(sm, jnp.concatenate([pk, pk, pk, pk], axis=0), 0.0)

    r0s, q, k, v, bcol, gam_col, tot_col, dec = [], [], [], [], [], [], [], []
    for hh, dirn, g in chains:
        r0 = pl.multiple_of(g * GROUP, GROUP)
        cols = slice(hh * dh, (hh + 1) * dh)
        r0s.append(r0)
        q.append(q_s[pl.ds(r0, GROUP), cols])
        k.append(k_s[pl.ds(r0, GROUP), cols])
        v.append(v_s[pl.ds(r0, GROUP), cols])
        cg = 2 * DN_HEADS * dirn + hh
        gcol = gb_ref[pl.ds(r0, GROUP), cg:cg + 1]
        bcol.append(gb_ref[pl.ds(r0, GROUP), cg + DN_HEADS:cg + DN_HEADS + 1])
        grow = gr_ref[hh, dirn, pl.ds(g, 1), :]
        sm, incl, _, incl_t = masks[dirn]
        gc = jnp.sum(jnp.where(incl, grow, 0.0), axis=1, keepdims=True)
        gr = jnp.sum(jnp.where(incl_t, gcol, 0.0), axis=0, keepdims=True)
        gam_col.append(gc)
        tot_col.append(jnp.sum(jnp.where(sm, grow, 0.0), axis=1, keepdims=True))
        dec.append(jnp.where(incl, jnp.exp(jnp.where(incl, gc - gr, 0.0)), 0.0))

    kq = [_dot_nt(jnp.concatenate([k[i], q[i]], axis=0), k[i]) for i in rng]
    mk_bd = [jnp.where(masks[chains[i][1]][2], -(bcol[i] * kq[i][:GROUP] * dec[i]), 0.0) for i in rng]
    attn = [kq[i][GROUP:] * dec[i] for i in rng]

    pi = lax.broadcasted_iota(jnp.int32, (CHUNK, GROUP), 0)
    pj = lax.broadcasted_iota(jnp.int32, (CHUNK, GROUP), 1)
    eye_pk = jnp.where(pi == (pj % CHUNK), 1.0, 0.0)
    mk_pk = [pack(mk_bd[i]) for i in rng]
    p = [eye_pk + mk_pk[i] for i in rng]
    mk_pk = [_bdot(mk_pk[i], mk_bd[i]) for i in rng]
    for _ in range(4):
        mk_b = [unpack(mk_pk[i], same[i]) for i in rng]
        both = [_bdot(jnp.concatenate([p[i], mk_pk[i]], axis=0), mk_b[i]) for i in rng]
        p = [p[i] + both[i][:CHUNK] for i in rng]
        mk_pk = [both[i][CHUNK:] for i in rng]
    p = [p[i] + _bdot(p[i], unpack(mk_pk[i], same[i])) for i in rng]
    t_bd = [unpack(p[i], same[i]) for i in rng]

    eg = [jnp.exp(gam_col[i]) for i in rng]
    rhs = [jnp.concatenate([v[i] * bcol[i], k[i] * (bcol[i] * eg[i])], axis=1) for i in rng]
    uw = [_bdot(t_bd[i], rhs[i]) for i in rng]
    qg = [q[i] * eg[i] for i in rng]
    kend = [k[i] * jnp.exp(tot_col[i] - gam_col[i]) for i in rng]
    gend = [jnp.exp(tot_col[i]) for i in rng]

    nc = GROUP // CHUNK
    v_new_c = [[None] * nc for _ in rng]
    o_state_c = [[None] * nc for _ in rng]
    for step in range(nc):
        cs = [step if chains[i][1] == 0 else nc - 1 - step for i in rng]
        lo = [c * CHUNK for c in cs]
        lhs = [jnp.concatenate([uw[i][lo[i]:lo[i] + CHUNK, dh:], qg[i][lo[i]:lo[i] + CHUNK]], axis=0)
               for i in rng]
        wq = [_bdot(lhs[i], states[i]) for i in rng]
        v_new = [uw[i][lo[i]:lo[i] + CHUNK, :dh] - wq[i][:CHUNK] for i in rng]
        states = [states[i] * gend[i][lo[i]:lo[i] + 1] + _dot_tn(kend[i][lo[i]:lo[i] + CHUNK], v_new[i])
                  for i in rng]
        for i in rng:
            v_new_c[i][cs[i]] = v_new[i]
            o_state_c[i][cs[i]] = wq[i][CHUNK:]
    for i in rng:
        o_all = (jnp.concatenate(o_state_c[i], axis=0)
                 + _bdot(attn[i], jnp.concatenate(v_new_c[i], axis=0)))
        hh = chains[i][0]
        o_refs[chains[i][1]][pl.ds(r0s[i], GROUP), hh * dh:(hh + 1) * dh] = o_all
    return states


def _dn_kernel(q_s, k_s, v_s, gb_ref, gr_ref, s0f_ref, s0b_ref, of_ref, ob_ref, sf_ref, sb_ref, *, seq_len):
    ng = seq_len // GROUP
    nh = DN_HEADS

    def body(m, carry):
        chains = [(hh, d, m if d == 0 else ng - 1 - m) for hh in range(nh) for d in (0, 1)]
        return tuple(_dn_groups(chains, list(carry), q_s, k_s, v_s, gb_ref, gr_ref, (of_ref, ob_ref)))

    init = tuple(r[hh] for hh in range(nh) for r in (s0f_ref, s0b_ref))
    fin = lax.fori_loop(0, ng, body, init)
    for hh in range(nh):
        sf_ref[hh] = fin[2 * hh]
        sb_ref[hh] = fin[2 * hh + 1]


def _deltanet(qkv, gb, grow, s0f, s0b, bsz, seq_len):
    h, dh = DN_HEADS, DN_HEAD_DIM
    ng = seq_len // GROUP
    w = h * dh
    once = pl.Buffered(1)
    col = lambda j: pl.BlockSpec((seq_len, w), lambda b: (b, j), pipeline_mode=once)
    st = pl.BlockSpec((None, h, dh, dh), lambda b: (b, 0, 0, 0))
    return pl.pallas_call(
        functools.partial(_dn_kernel, seq_len=seq_len),
        grid=(bsz,),
        in_specs=[col(0), col(1), col(2),
                  pl.BlockSpec((seq_len, 4 * h), lambda b: (b, 0), pipeline_mode=once),
                  pl.BlockSpec((None, h, 2, ng, GROUP), lambda b: (b, 0, 0, 0, 0)),
                  st, st],
        out_specs=[col(0), col(0), st, st],
        out_shape=[jax.ShapeDtypeStruct((bsz * seq_len, w), F32),
                   jax.ShapeDtypeStruct((bsz * seq_len, w), F32),
                   jax.ShapeDtypeStruct((bsz, h, dh, dh), F32),
                   jax.ShapeDtypeStruct((bsz, h, dh, dh), F32)],
        compiler_params=_cparams("parallel"),
        name="deltanet",
    )(qkv, qkv, qkv, gb, grow, s0f, s0b)


def _dft_table_kernel(tac_ref, tas_ref, tbc_ref, tbs_ref, cp_ref, sp_ref, *, nt1):
    tbc = tbc_ref[...]
    tbs = tbs_ref[...]
    for t1 in range(nt1):
        ac = tac_ref[:, t1:t1 + 1]
        asn = tas_ref[:, t1:t1 + 1]
        cp_ref[:, t1 * LANES:(t1 + 1) * LANES] = (ac * tbc - asn * tbs).astype(BF16)
        sp_ref[:, t1 * LANES:(t1 + 1) * LANES] = (asn * tbc + ac * tbs).astype(BF16)


def _dft_tables(n):
    nt1 = n // LANES
    s = np.arange(n, dtype=np.int64)[:, None]
    t1 = np.arange(nt1, dtype=np.int64)[None, :]
    t0 = np.arange(LANES, dtype=np.int64)[None, :]
    ang_a = 2.0 * np.pi * ((s * t1) % nt1) / nt1
    ang_b = 2.0 * np.pi * ((s * t0) % n) / n
    pad = ((0, 0), (0, LANES - nt1))
    tac = jnp.asarray(np.pad(np.cos(ang_a), pad), F32)
    tas = jnp.asarray(np.pad(np.sin(ang_a), pad), F32)
    tbc = jnp.asarray(np.cos(ang_b), F32)
    tbs = jnp.asarray(np.sin(ang_b), F32)
    tr = min(n, 256)
    small = pl.BlockSpec((tr, LANES), lambda i: (i, 0))
    big = pl.BlockSpec((tr, n), lambda i: (i, 0))
    return pl.pallas_call(
        functools.partial(_dft_table_kernel, nt1=nt1),
        grid=(n // tr,),
        in_specs=[small, small, small, small],
        out_specs=[big, big],
        out_shape=[jax.ShapeDtypeStruct((n, n), BF16)] * 2,
        compiler_params=_cparams("parallel"),
        name="dft_tables",
    )(tac, tas, tbc, tbs)


MIRROR_HEAD = 2 * SUBLANES


def _fourier_kernel(f_ref, p1_ref, p2_ref, mid_ref, cp_ref, sp_ref, cc_ref, sc_ref, o_ref):
    kstep = pl.program_id(1)
    tk = f_ref.shape[0]
    seq_len = o_ref.shape[0]
    cc = cc_ref[...]
    sc = sc_ref[...]

    def chan(xb, tab):
        return jnp.concatenate([jnp.dot(xb[:, g * LANES:(g + 1) * LANES], tab, preferred_element_type=F32)
                                for g in range(FOURIER_GROUPS)], axis=1)

    xb = f_ref[...].astype(BF16)
    pb = jnp.concatenate([p1_ref[...], p2_ref[...]], axis=0).astype(BF16)
    ri = lax.broadcasted_iota(jnp.int32, (tk, tk + MIRROR_HEAD), 0)
    ci = lax.broadcasted_iota(jnp.int32, (tk, tk + MIRROR_HEAD), 1)
    sel = ((ri >= 1) & (ci == tk - ri)) | ((ri == 0) & (ci == tk) & (kstep > 0))
    xr = jnp.dot(jnp.where(sel, 1.0, 0.0).astype(BF16), pb, preferred_element_type=F32)
    xf = xb.astype(F32)
    zc = chan((xf + xr).astype(BF16), cc).astype(BF16)
    zs = chan((xf - xr).astype(BF16), sc).astype(BF16)
    part = (jnp.dot(cp_ref[...], zc, preferred_element_type=F32)
            - jnp.dot(sp_ref[...], zs, preferred_element_type=F32))

    @pl.when(kstep == 0)
    def _():
        zmid = chan(mid_ref[...].astype(BF16), cc)[0:1, :]
        s_idx = lax.broadcasted_iota(jnp.int32, (seq_len, 1), 0)
        o_ref[...] = part + jnp.where(s_idx % 2 == 0, 1.0, -1.0) * zmid

    @pl.when(kstep > 0)
    def _():
        o_ref[...] += part


def _fourier(f, cp, sp, cc, sc, bsz, seq_len):
    t, fw = f.shape
    tk = min(seq_len // 2, 512)
    nk = seq_len // tk
    hpb = tk // MIRROR_HEAD
    last_head = t // MIRROR_HEAD - 1
    return pl.pallas_call(
        _fourier_kernel,
        grid=(bsz, nk // 2),
        in_specs=[pl.BlockSpec((tk, fw), lambda b, k: (b * nk + k, 0)),
                  pl.BlockSpec((tk, fw), lambda b, k: (b * nk + nk - 1 - k, 0)),
                  pl.BlockSpec((MIRROR_HEAD, fw), lambda b, k: (jnp.minimum((b * nk + nk - k) * hpb, last_head), 0)),
                  pl.BlockSpec((MIRROR_HEAD, fw), lambda b, k: ((b * nk + nk // 2) * hpb, 0)),
                  pl.BlockSpec((seq_len, tk), lambda b, k: (0, k)),
                  pl.BlockSpec((seq_len, tk), lambda b, k: (0, k)),
                  pl.BlockSpec((LANES, LANES), lambda b, k: (0, 0)),
                  pl.BlockSpec((LANES, LANES), lambda b, k: (0, 0))],
        out_specs=pl.BlockSpec((seq_len, fw), lambda b, k: (b, 0), pipeline_mode=pl.Buffered(1)),
        out_shape=jax.ShapeDtypeStruct((bsz * seq_len, fw), F32),
        compiler_params=_cparams("parallel", "arbitrary"),
        name="fourier",
    )(f, f, f, f, cp, sp, cc, sc)


def _outproj_kernel(four_ref, of_ref, ob_ref, z_ref, x_ref, gout_ref, gt1_ref, sc2_ref, sh2_ref, g2_ref,
                    wout_ref, wr_ref, br_ref, cnt0_ref, xn_ref, h2_ref, ti_ref, tg_ref, rk_ref, cnt_ref,
                    carry, *, fw):
    @pl.when(pl.program_id(0) == 0)
    def _():
        carry[...] = cnt0_ref[...]

    o = of_ref[...] + ob_ref[...]
    z = z_ref[...]
    gout = gout_ref[...]
    parts = []
    for hh in range(DN_HEADS):
        sl = slice(hh * DN_HEAD_DIM, (hh + 1) * DN_HEAD_DIM)
        oh = o[:, sl]
        zh = z[:, sl]
        oh = oh * lax.rsqrt(jnp.mean(oh * oh, axis=-1, keepdims=True) + EPS) * gout
        parts.append(oh * (zh * _sigmoid(zh)))
    gated = jnp.concatenate(parts, axis=1).astype(BF16)
    mix = (jnp.dot(four_ref[...].astype(BF16), wout_ref[:fw, :], preferred_element_type=F32)
           + jnp.dot(gated, wout_ref[fw:, :], preferred_element_type=F32))
    xn = x_ref[...] + gt1_ref[...] * mix
    xn_ref[...] = xn
    inv = lax.rsqrt(jnp.mean(xn * xn, axis=-1, keepdims=True) + EPS)
    h2 = (xn * inv * g2_ref[...]) * (1.0 + sc2_ref[...]) + sh2_ref[...]
    tm = h2.shape[0]
    for s in range(h2.shape[1] // LANES):
        h2_ref[pl.ds(s, tm, stride=SUBLANES), :] = h2[:, s * LANES:(s + 1) * LANES]
    wr = wr_ref[...]
    wr_hi = wr.astype(BF16)
    wr_lo = (wr - wr_hi.astype(F32)).astype(BF16)
    h2_hi = h2.astype(BF16)
    h2_lo = (h2 - h2_hi.astype(F32)).astype(BF16)
    logits = (jnp.dot(h2_hi, wr_hi, preferred_element_type=F32) + jnp.dot(h2_hi, wr_lo, preferred_element_type=F32)
              + jnp.dot(h2_lo, wr_hi, preferred_element_type=F32)) + br_ref[...]
    lane = lax.broadcasted_iota(jnp.int32, logits.shape, 1)
    neg = jnp.float32(-jnp.inf)
    logits = jnp.where(lane < N_EXPERTS, logits, neg)
    vals, idxs = [], []
    for _ in range(TOP_K):
        m = jnp.max(logits, axis=-1, keepdims=True)
        idx = jnp.min(jnp.where(logits == m, lane, LANES), axis=-1, keepdims=True)
        vals.append(m)
        idxs.append(idx)
        logits = jnp.where(lane == idx, neg, logits)
    ex = [jnp.exp(vv - vals[0]) for vv in vals]
    den = ex[0] + ex[1] + ex[2] + ex[3]
    onehot = jnp.zeros(logits.shape, F32)
    for j in range(TOP_K):
        onehot = onehot + jnp.where(lane == idxs[j], 1.0, 0.0)
    ri = lax.broadcasted_iota(jnp.int32, (tm, tm), 0)
    ci = lax.broadcasted_iota(jnp.int32, (tm, tm), 1)
    tri = jnp.where(ri > ci, 1.0, 0.0).astype(BF16)
    base = jnp.dot(tri, onehot.astype(BF16), preferred_element_type=F32) + carry[...]
    carry[...] = carry[...] + jnp.sum(onehot, axis=0, keepdims=True)
    cnt_ref[...] = carry[...]
    kl = lax.broadcasted_iota(jnp.int32, (tm, TOP_K), 1)
    ti = jnp.zeros((tm, TOP_K), jnp.int32)
    tg = jnp.zeros((tm, TOP_K), F32)
    rk = jnp.zeros((tm, TOP_K), jnp.int32)
    for j in range(TOP_K):
        rank_j = jnp.sum(jnp.where(lane == idxs[j], base, 0.0), axis=-1, keepdims=True).astype(jnp.int32)
        ti = jnp.where(kl == j, idxs[j], ti)
        tg = jnp.where(kl == j, ex[j] / den, tg)
        rk = jnp.where(kl == j, rank_j, rk)
    ti_ref[...] = ti
    tg_ref[...] = tg
    rk_ref[...] = rk


def _outproj(four, o_f, o_b, z, x, g_out, gt1, sc2, sh2, g2, w_out, w_r, b_r, cnt0, rows_per_group, tm=256):
    t, d = x.shape
    fw = four.shape[1]
    ng = gt1.shape[0]
    tpg = rows_per_group // tm
    grp = lambda i: (jnp.minimum(i // tpg, ng - 1), 0, 0)
    const = lambda i: (0, 0)
    row = lambda w: pl.BlockSpec((tm, w), lambda i: (i, 0))
    gspec = pl.BlockSpec((None, 1, d), grp)
    return pl.pallas_call(
        functools.partial(_outproj_kernel, fw=fw),
        grid=(t // tm,),
        in_specs=[row(fw), row(z.shape[1]), row(z.shape[1]), row(z.shape[1]), row(d),
                  pl.BlockSpec((1, DN_HEAD_DIM), const), gspec, gspec, gspec,
                  pl.BlockSpec((1, d), const),
                  pl.BlockSpec(w_out.shape, const),
                  pl.BlockSpec(w_r.shape, const),
                  pl.BlockSpec((1, LANES), const),
                  pl.BlockSpec((1, LANES), const)],
        out_specs=[row(d),
                   pl.BlockSpec((tm * SUBLANES, LANES), lambda i: (i, 0)),
                   row(TOP_K), row(TOP_K), row(TOP_K),
                   pl.BlockSpec((1, LANES), const)],
        out_shape=[jax.ShapeDtypeStruct((t, d), F32),
                   jax.ShapeDtypeStruct((t * d // LANES, LANES), F32),
                   jax.ShapeDtypeStruct((t, TOP_K), jnp.int32),
                   jax.ShapeDtypeStruct((t, TOP_K), F32),
                   jax.ShapeDtypeStruct((t, TOP_K), jnp.int32),
                   jax.ShapeDtypeStruct((1, LANES), F32)],
        scratch_shapes=[pltpu.VMEM((1, LANES), F32)],
        compiler_params=_cparams("arbitrary"),
        name="outproj",
    )(four, o_f, o_b, z, x, g_out.reshape(1, DN_HEAD_DIM), gt1, sc2, sh2, g2.reshape(1, d), w_out, w_r, b_r, cnt0)


TOK_UNROLL = 4


def _row_copy(src, src_row, dst, dst_row, sem):
    return pltpu.make_async_copy(
        src.at[pl.ds(pl.multiple_of(src_row * SUBLANES, SUBLANES), SUBLANES), :],
        dst.at[pl.ds(pl.multiple_of(dst_row * SUBLANES, SUBLANES), SUBLANES), :], sem)


def _dispatch_kernel(cnt_ref, ps_ref, pd_ref, nu_ref, *refs, tm, steps, n_blocks):
    ns = len(steps)
    dest_refs, h_refs = refs[0:2 * ns:2], refs[1:2 * ns:2]
    xs_hbm, zero_s, sem, sem_pad = refs[2 * ns:]
    i = pl.program_id(0)
    blk_rows = MOE_BLOCK * SUBLANES

    def tail_copy(b):
        return pltpu.make_async_copy(zero_s, xs_hbm.at[pl.ds(pl.multiple_of(b * blk_rows, blk_rows), blk_rows), :],
                                     sem_pad)

    @pl.when(i == 0)
    def _():
        zero_s[...] = jnp.zeros(zero_s.shape, F32)

        def per_expert(e, tot):
            def one(r, c):
                _row_copy(zero_s, 0, xs_hbm, ps_ref[e] + r, sem_pad).start()
                return c

            lax.fori_loop(cnt_ref[e], pd_ref[e], one, 0)
            return tot + pd_ref[e] - cnt_ref[e]

        n_pad = lax.fori_loop(0, N_EXPERTS, per_expert, 0)

        def start_tail(b, c):
            tail_copy(b).start()
            return c

        lax.fori_loop(nu_ref[0], n_blocks, start_tail, 0)

        def wait_row(r, c):
            _row_copy(zero_s, 0, xs_hbm, 0, sem_pad).wait()
            return c

        lax.fori_loop(0, n_pad, wait_row, 0)

        def wait_tail(b, c):
            tail_copy(b).wait()
            return c

        lax.fori_loop(nu_ref[0], n_blocks, wait_tail, 0)

    first = 0
    for dest_ref, h_ref, nst in zip(dest_refs, h_refs, steps):
        @pl.when((i >= first) & (i < first + nst))
        def _(dest_ref=dest_ref, h_ref=h_ref):
            def issue(blk, carry):
                for tt in range(TOK_UNROLL):
                    t = blk * TOK_UNROLL + tt
                    for j in range(TOP_K):
                        _row_copy(h_ref, t, xs_hbm, dest_ref[0, 0, t * TOP_K + j], sem).start(priority=j % 2)
                return carry

            lax.fori_loop(0, tm // TOK_UNROLL, issue, 0)
            for _ in range(TOP_K):
                pltpu.make_async_copy(h_ref, xs_hbm.at[pl.ds(0, tm * SUBLANES), :], sem).wait()

        first += nst


def _dispatch(streams, counts, pstart, padded, n_used, n_blocks, tm=512):
    seg = streams[0][0].shape[0] // streams[0][1].shape[0]
    steps = [dest.shape[0] // tm for _, dest in streams]
    in_specs, args, first = [], [], 0
    for (h_rt, dest), nst in zip(streams, steps):
        idx = lambda i, *_, first=first, nst=nst: (jnp.clip(i - first, 0, nst - 1), 0)
        idx3 = lambda i, *_, idx=idx: idx(i) + (0,)
        in_specs += [pl.BlockSpec((1, 1, tm * TOP_K), idx3, memory_space=pltpu.SMEM),
                     pl.BlockSpec((tm * seg, LANES), idx)]
        args += [dest.reshape(nst, 1, tm * TOP_K), h_rt]
        first += nst
    grid_spec = pltpu.PrefetchScalarGridSpec(
        num_scalar_prefetch=4,
        grid=(first,),
        in_specs=in_specs,
        out_specs=pl.BlockSpec(memory_space=pltpu.HBM),
        scratch_shapes=[pltpu.VMEM((MOE_BLOCK * seg, LANES), F32),
                        pltpu.SemaphoreType.DMA(()), pltpu.SemaphoreType.DMA(())],
    )
    return pl.pallas_call(
        functools.partial(_dispatch_kernel, tm=tm, steps=tuple(steps), n_blocks=n_blocks),
        grid_spec=grid_spec,
        out_shape=jax.ShapeDtypeStruct((n_blocks * MOE_BLOCK * seg, LANES), F32),
        compiler_params=_cparams("arbitrary"),
        name="dispatch",
    )(counts, pstart, padded, n_used, *args)


def _expert_kernel(be_ref, nu_ref, xs_ref, wg_ref, bg_ref, wu_ref, bu_ref, wd_ref, bd_ref, y_ref,
                   x2d, wg_b, wu_b, wd_b):
    i = pl.program_id(0)
    rows = MOE_BLOCK
    nseg = x2d.shape[1] // LANES

    @pl.when(i < nu_ref[0])
    def _():
        @pl.when((i == 0) | (be_ref[i] != be_ref[jnp.maximum(i - 1, 0)]))
        def _():
            wg_b[...] = wg_ref[...].astype(BF16)
            wu_b[...] = wu_ref[...].astype(BF16)
            wd_b[...] = wd_ref[...].astype(BF16)

        for s in range(nseg):
            x2d[:, s * LANES:(s + 1) * LANES] = xs_ref[pl.ds(s, rows, stride=SUBLANES), :].astype(BF16)
        x = x2d[...]
        a = jnp.minimum(jnp.dot(x, wg_b[...], preferred_element_type=F32) + bg_ref[...], SWIGLU_LIMIT)
        u = jnp.clip(jnp.dot(x, wu_b[...], preferred_element_type=F32) + bu_ref[...], -SWIGLU_LIMIT, SWIGLU_LIMIT)
        act = (a * _sigmoid(SWIGLU_ALPHA * a) * (u + 1.0)).astype(BF16)
        y = jnp.dot(act, wd_b[...], preferred_element_type=F32) + bd_ref[...]
        for s in range(nseg):
            y_ref[pl.ds(s, rows, stride=SUBLANES), :] = y[:, s * LANES:(s + 1) * LANES]

    @pl.when(i >= nu_ref[0])
    def _():
        y_ref[...] = jnp.zeros(y_ref.shape, F32)


def _experts(xs, block_e, n_used, layer, w_gate, b_gate, w_up, b_up, w_down, b_down):
    n_blocks = block_e.shape[0]
    depth, ne, d, ff = w_gate.shape
    seg = d // LANES
    blk = lambda i, nu: jnp.minimum(i, jnp.maximum(nu[0] - 1, 0))
    wspec = lambda shp: pl.BlockSpec((None, None) + shp, lambda i, be, nu: (layer, be[blk(i, nu)], 0, 0))
    grid_spec = pltpu.PrefetchScalarGridSpec(
        num_scalar_prefetch=2,
        grid=(n_blocks,),
        in_specs=[pl.BlockSpec((MOE_BLOCK * seg, LANES), lambda i, be, nu: (blk(i, nu), 0)),
                  wspec((d, ff)), wspec((1, ff)), wspec((d, ff)), wspec((1, ff)), wspec((ff, d)), wspec((1, d))],
        out_specs=pl.BlockSpec((MOE_BLOCK * seg, LANES), lambda i, be, nu: (i, 0)),
        scratch_shapes=[pltpu.VMEM((MOE_BLOCK, d), BF16), pltpu.VMEM((d, ff), BF16), pltpu.VMEM((d, ff), BF16),
                        pltpu.VMEM((ff, d), BF16)],
    )
    return pl.pallas_call(
        _expert_kernel,
        grid_spec=grid_spec,
        out_shape=jax.ShapeDtypeStruct((n_blocks * MOE_BLOCK * seg, LANES), F32),
        compiler_params=_cparams("arbitrary"),
        name="experts",
    )(block_e, n_used, xs, w_gate, b_gate.reshape(depth, ne, 1, ff), w_up, b_up.reshape(depth, ne, 1, ff),
      w_down, b_down.reshape(depth, ne, 1, d))


def _combine_kernel(dest_ref, dnext_ref, y_hbm, x_ref, tg_ref, gt_ref, gf_ref, o_ref, buf0, buf1, sem0, sem1, *,
                    tm, final_norm):
    n = tm * TOP_K
    i = pl.program_id(0)
    last = pl.num_programs(0) - 1

    def gather(d_ref, buf, sem):
        def issue(blk, carry):
            for tt in range(TOK_UNROLL):
                t = blk * TOK_UNROLL + tt
                for j in range(TOP_K):
                    _row_copy(y_hbm, d_ref[0, 0, t * TOP_K + j], buf, j * tm + t, sem).start(priority=j % 2)
            return carry

        lax.fori_loop(0, tm // TOK_UNROLL, issue, 0)

    def reduce(buf, sem):
        pltpu.make_async_copy(y_hbm.at[pl.ds(0, n * SUBLANES), :], buf, sem).wait()
        gt = gt_ref[...]
        tg = tg_ref[...]
        d = o_ref.shape[1]
        ssq = jnp.zeros((tm, 1), F32)
        for s in range(d // LANES):
            acc = tg[:, 0:1] * buf[pl.ds(s, tm, stride=SUBLANES), :]
            for j in range(1, TOP_K):
                acc = acc + tg[:, j:j + 1] * buf[pl.ds(j * tm * SUBLANES + s, tm, stride=SUBLANES), :]
            sl = slice(s * LANES, (s + 1) * LANES)
            o = x_ref[:, sl] + gt[:, sl] * acc
            o_ref[:, sl] = o
            if final_norm:
                ssq = ssq + jnp.sum(o * o, axis=-1, keepdims=True)
        if final_norm:
            inv = lax.rsqrt(ssq / d + EPS)
            o_ref[...] = o_ref[...] * inv * gf_ref[...]

    @pl.when(i == 0)
    def _():
        gather(dest_ref, buf0, sem0)

    for par, (cur, nxt) in enumerate((((buf0, sem0), (buf1, sem1)), ((buf1, sem1), (buf0, sem0)))):
        @pl.when(i % 2 == par)
        def _(cur=cur, nxt=nxt):
            @pl.when(i < last)
            def _():
                gather(dnext_ref, *nxt)

            reduce(*cur)


def _combine(x, y_rt, dest, top_g, gt2, g_final, final_norm, rows_per_group, tm=256):
    t, d = x.shape
    ng = gt2.shape[0]
    nt = t // tm
    tpg = rows_per_group // tm
    dest3 = dest.reshape(nt, 1, tm * TOP_K)
    buf = pltpu.VMEM((tm * TOP_K * SUBLANES, LANES), F32)
    return pl.pallas_call(
        functools.partial(_combine_kernel, tm=tm, final_norm=final_norm),
        grid=(nt,),
        in_specs=[pl.BlockSpec((1, 1, tm * TOP_K), lambda i: (i, 0, 0), memory_space=pltpu.SMEM),
                  pl.BlockSpec((1, 1, tm * TOP_K), lambda i: (jnp.minimum(i + 1, nt - 1), 0, 0),
                               memory_space=pltpu.SMEM),
                  pl.BlockSpec(memory_space=pltpu.HBM),
                  pl.BlockSpec((tm, d), lambda i: (i, 0)),
                  pl.BlockSpec((tm, TOP_K), lambda i: (i, 0)),
                  pl.BlockSpec((None, 1, d), lambda i: (jnp.minimum(i // tpg, ng - 1), 0, 0)),
                  pl.BlockSpec((1, d), lambda i: (0, 0))],
        out_specs=pl.BlockSpec((tm, d), lambda i: (i, 0)),
        out_shape=jax.ShapeDtypeStruct((t, d), F32),
        scratch_shapes=[buf, buf, pltpu.SemaphoreType.DMA(()), pltpu.SemaphoreType.DMA(())],
        compiler_params=_cparams("arbitrary"),
        name="combine",
    )(dest3, dest3, y_rt, x, top_g, gt2, g_final.reshape(1, d))


def _slot_tables(cnt_row, n_tok):
    n_blocks = -(-n_tok * TOP_K // MOE_BLOCK) + N_EXPERTS
    counts = cnt_row[0, :N_EXPERTS].astype(jnp.int32)
    padded = (counts + MOE_BLOCK - 1) // MOE_BLOCK * MOE_BLOCK
    padded_end = jnp.cumsum(padded)
    pstart = padded_end - padded
    blk_start = jnp.arange(n_blocks, dtype=jnp.int32) * MOE_BLOCK
    block_e = jnp.minimum(jnp.sum(padded_end[None, :] <= blk_start[:, None], axis=1), N_EXPERTS - 1).astype(jnp.int32)
    n_used = (padded_end[-1:] // MOE_BLOCK).astype(jnp.int32)
    return counts, pstart.astype(jnp.int32), padded.astype(jnp.int32), block_e, n_used, n_blocks


def _slot_of(top_i, rank, pstart):
    e = jnp.arange(N_EXPERTS, dtype=jnp.int32)
    return rank + jnp.sum(jnp.where(top_i[..., None] == e, pstart, 0), axis=-1)


def _decay_rows(gb, bsz, seq_len):
    g5 = gb.reshape(bsz, seq_len, 2, 2, DN_HEADS)
    return g5[:, :, :, 0, :].transpose(0, 3, 2, 1).reshape(bsz, DN_HEADS, 2, seq_len // GROUP, GROUP)


def kernel(x, c, ctx, c_ctx, w_mod, b_mod, g_norm1, w_in, conv_w, a_log, dt_bias, g_out_norm, w_out, g_norm2,
           w_router, b_router, w_gate, b_gate, w_up, b_up, w_down, b_down, g_final):
    bsz, seq, d = x.shape
    clen = ctx.shape[1]
    depth = w_mod.shape[0]
    fw = FOURIER_GROUPS * LANES
    nmain = fw + 4 * DN_HEADS * DN_HEAD_DIM
    nab = 4 * DN_HEADS

    xt = x.reshape(bsz * seq, d)
    ct = ctx.reshape(bsz * clen, d)
    c_rows = jnp.concatenate([c, c_ctx[None, :], jnp.zeros((2 * SUBLANES - bsz - 1, d), F32)], axis=0)

    cp_x, sp_x = _dft_tables(seq)
    cp_c, sp_c = _dft_tables(clen)
    ch = np.arange(LANES, dtype=np.int64)
    ang = 2.0 * np.pi * ((ch[:, None] * ch[None, :]) % LANES) / LANES

    def chan(tab, n):
        return jnp.asarray(tab / math.sqrt(n * LANES), F32).astype(BF16)

    lanepad = lambda v: jnp.pad(v, ((0, 0), (0, LANES - v.shape[1])))
    zeros4 = jnp.zeros((DN_HEADS,), F32)
    isdec_row = lanepad(jnp.tile(jnp.concatenate([jnp.ones((DN_HEADS,), F32), zeros4]), 2)[None, :])

    for l in range(depth):
        update_ctx = l < depth - 1
        mod = _mod(c_rows, w_mod[l], b_mod[l]).reshape(2 * SUBLANES, 6, 1, d)
        mx = lambda j: mod[:bsz, j]
        mc = lambda j: mod[bsz:bsz + 1, j]

        w_main = w_in[l][:, :nmain].astype(BF16)
        w_ab = lanepad(w_in[l][:, nmain:]).astype(BF16)
        alog_row = lanepad(jnp.concatenate([a_log[l][0], zeros4, a_log[l][1], zeros4])[None, :])
        dtb_row = lanepad(jnp.concatenate([dt_bias[l][0], zeros4, dt_bias[l][1], zeros4])[None, :])
        ip = lambda tok, sh, sc, rpg, slen: _inproj(tok, g_norm1[l], sc, sh, w_main, w_ab, conv_w[l], alog_row,
                                                     dtb_row, isdec_row, rpg, slen)
        f_x, qkv_x, z_x, gb_x = ip(xt, mx(0), mx(1), seq, seq)
        f_c, qkv_c, z_c, gb_c = ip(ct, mc(0), mc(1), bsz * clen, clen)

        s_zero = jnp.zeros((bsz, DN_HEADS, DN_HEAD_DIM, DN_HEAD_DIM), F32)
        ocf, ocb, s_f, s_b = _deltanet(qkv_c, gb_c, _decay_rows(gb_c, bsz, clen), s_zero, s_zero, bsz, clen)
        oxf, oxb, _, _ = _deltanet(qkv_x, gb_x, _decay_rows(gb_x, bsz, seq), s_f, s_b, bsz, seq)

        w_out_b = w_out[l].astype(BF16)
        w_r = lanepad(w_router[l])
        b_r = lanepad(b_router[l][None, :])
        four_x = _fourier(f_x, cp_x, sp_x, chan(np.cos(ang), seq), chan(np.sin(ang), seq), bsz, seq)
        cnt0 = jnp.zeros((1, LANES), F32)
        xn_x, h2_x, ti_x, tg_x, rk_x, cnt = _outproj(four_x, oxf, oxb, z_x, xt, g_out_norm[l], mx(2), mx(4), mx(3),
                                                      g_norm2[l], w_out_b, w_r, b_r, cnt0, seq)
        n_tok = bsz * seq
        if update_ctx:
            four_c = _fourier(f_c, cp_c, sp_c, chan(np.cos(ang), clen), chan(np.sin(ang), clen), bsz, clen)
            xn_c, h2_c, ti_c, tg_c, rk_c, cnt = _outproj(four_c, ocf, ocb, z_c, ct, g_out_norm[l], mc(2), mc(4),
                                                          mc(3), g_norm2[l], w_out_b, w_r, b_r, cnt, bsz * clen)
            n_tok += bsz * clen

        counts, pstart, padded, block_e, n_used, n_blocks = _slot_tables(cnt, n_tok)
        dest_x = _slot_of(ti_x, rk_x, pstart)
        streams = [(h2_x, dest_x)]
        if update_ctx:
            dest_c = _slot_of(ti_c, rk_c, pstart)
            streams.append((h2_c, dest_c))
        xs = _dispatch(streams, counts, pstart, padded, n_used, n_blocks)
        y_rt = _experts(xs, block_e, n_used, l, w_gate, b_gate, w_up, b_up, w_down, b_down)
        xt = _combine(xn_x, y_rt, dest_x, tg_x, mx(5), g_final, l == depth - 1, seq)
        if update_ctx:
            ct = _combine(xn_c, y_rt, dest_c, tg_c, mc(5), g_final, False, bsz * clen)

    return xt.reshape(bsz, seq, d)
```

```python
import functools
import math

import numpy as np
import jax
import jax.numpy as jnp
from jax import lax
from jax.experimental import pallas as pl
from jax.experimental.pallas import tpu as pltpu

FOURIER_GROUPS = 4
DN_HEADS = 4
DN_HEAD_DIM = 128
CONV_K = 5
N_EXPERTS = 32
TOP_K = 4
SWIGLU_LIMIT = 7.0
SWIGLU_ALPHA = 1.702
EPS = 1e-6

LANES = 128
SUBLANES = 8
V7X_VMEM_BYTES = 64 * 1024 * 1024
VMEM_LIMIT = V7X_VMEM_BYTES - 8 * 1024 * 1024
CHUNK = 64
GROUP = 4 * CHUNK
MOE_BLOCK = 256

F32 = jnp.float32
BF16 = jnp.bfloat16


def _cparams(*sem):
    return pltpu.CompilerParams(dimension_semantics=sem, vmem_limit_bytes=VMEM_LIMIT)


def _sigmoid(x):
    return 1.0 / (1.0 + jnp.exp(-x))


def _mod_kernel(c_ref, w_ref, b_ref, o_ref):
    c = c_ref[...]
    a = c * _sigmoid(c)
    o_ref[...] = jnp.dot(a, w_ref[...], preferred_element_type=F32) + b_ref[...]


def _mod(c_rows, w_mod, b_mod):
    r, d = c_rows.shape
    n = w_mod.shape[1]
    tn = n // 4
    return pl.pallas_call(
        _mod_kernel,
        grid=(n // tn,),
        in_specs=[pl.BlockSpec((r, d), lambda j: (0, 0)),
                  pl.BlockSpec((d, tn), lambda j: (0, j)),
                  pl.BlockSpec((1, tn), lambda j: (0, j))],
        out_specs=pl.BlockSpec((r, tn), lambda j: (0, j)),
        out_shape=jax.ShapeDtypeStruct((r, n), F32),
        compiler_params=_cparams("parallel"),
        name="mod",
    )(c_rows, w_mod, b_mod.reshape(1, n))


def _inproj_kernel(x_ref, xp_ref, xn_ref, g_ref, sc_ref, sh_ref, w_ref, wab_ref, cw_ref, alog_ref, dtb_ref,
                   isdec_ref, f_ref, qkv_ref, z_ref, gb_ref, *, fw, qw, seq_len):
    i = pl.program_id(0)
    tm = x_ref.shape[0]
    halo = SUBLANES

    def norm_mod(x):
        inv = lax.rsqrt(jnp.mean(x * x, axis=-1, keepdims=True) + EPS)
        return (x * inv * g_ref[...]) * (1.0 + sc_ref[...]) + sh_ref[...]

    h = norm_mod(x_ref[...])
    hb = h.astype(BF16)
    at_start = (i * tm) % seq_len == 0
    at_end = ((i + 1) * tm) % seq_len == 0
    hp = jnp.where(at_start, 0.0, norm_mod(xp_ref[...]))
    hn = jnp.where(at_end, 0.0, norm_mod(xn_ref[...]))
    hext = jnp.concatenate([hp, h, hn], axis=0).astype(BF16)
    f_ref[...] = jnp.dot(hb, w_ref[:, :fw], preferred_element_type=F32)
    n = tm + 2 * halo
    cwid = 2 * DN_HEAD_DIM
    for c in range(3 * qw // cwid):
        lo = c * cwid
        pre = jnp.dot(hext, w_ref[:, fw + lo:fw + lo + cwid], preferred_element_type=F32)
        acc = None
        for j in range(CONV_K):
            off = halo - CONV_K // 2 + j
            term = cw_ref[j:j + 1, lo:lo + cwid] * pltpu.roll(pre, n - off, 0)[:tm]
            acc = term if acc is None else acc + term
        y = acc * _sigmoid(acc)
        if lo < 2 * qw:
            scale = DN_HEAD_DIM ** -0.5 if lo < qw else 1.0
            segs = []
            for hh in range(cwid // DN_HEAD_DIM):
                ys = y[:, hh * DN_HEAD_DIM:(hh + 1) * DN_HEAD_DIM]
                segs.append(ys * (lax.rsqrt(jnp.sum(ys * ys, axis=-1, keepdims=True) + EPS) * scale))
            y = jnp.concatenate(segs, axis=1)
        qkv_ref[:, lo:lo + cwid] = y
    lo = fw + 3 * qw
    z_ref[...] = jnp.dot(hb, w_ref[:, lo:lo + qw], preferred_element_type=F32)
    ab = jnp.dot(hb, wab_ref[...], preferred_element_type=F32)
    pre = ab + dtb_ref[...]
    softplus = jnp.maximum(pre, 0.0) + jnp.log(1.0 + jnp.exp(-jnp.abs(pre)))
    dec = -jnp.exp(alog_ref[...]) * softplus
    gb = jnp.where(isdec_ref[...] > 0.0, dec, _sigmoid(ab))
    gb_ref[...] = gb[:, :gb_ref.shape[1]]


def _inproj(x, g1, sc, sh, w_main, w_ab, conv_w, alog_row, dtb_row, isdec_row, rows_per_group, seq_len):
    t, d = x.shape
    ng = sc.shape[0]
    fw = FOURIER_GROUPS * LANES
    qw = DN_HEADS * DN_HEAD_DIM
    nab = 4 * DN_HEADS
    tm = min(512, seq_len)
    hpt = tm // SUBLANES
    tpg = rows_per_group // tm
    grp = lambda i: (jnp.minimum(i // tpg, ng - 1), 0, 0)
    const = lambda i: (0, 0)
    return pl.pallas_call(
        functools.partial(_inproj_kernel, fw=fw, qw=qw, seq_len=seq_len),
        grid=(t // tm,),
        in_specs=[pl.BlockSpec((tm, d), lambda i: (i, 0)),
                  pl.BlockSpec((SUBLANES, d), lambda i: (jnp.maximum(i * hpt - 1, 0), 0)),
                  pl.BlockSpec((SUBLANES, d), lambda i: (jnp.minimum((i + 1) * hpt, t // SUBLANES - 1), 0)),
                  pl.BlockSpec((1, d), const),
                  pl.BlockSpec((None, 1, d), grp),
                  pl.BlockSpec((None, 1, d), grp),
                  pl.BlockSpec(w_main.shape, const),
                  pl.BlockSpec(w_ab.shape, const),
                  pl.BlockSpec(conv_w.shape, const),
                  pl.BlockSpec((1, LANES), const),
                  pl.BlockSpec((1, LANES), const),
                  pl.BlockSpec((1, LANES), const)],
        out_specs=[pl.BlockSpec((tm, fw), lambda i: (i, 0)),
                   pl.BlockSpec((tm, 3 * qw), lambda i: (i, 0)),
                   pl.BlockSpec((tm, qw), lambda i: (i, 0)),
                   pl.BlockSpec((tm, nab), lambda i: (i, 0))],
        out_shape=[jax.ShapeDtypeStruct((t, fw), F32),
                   jax.ShapeDtypeStruct((t, 3 * qw), F32),
                   jax.ShapeDtypeStruct((t, qw), F32),
                   jax.ShapeDtypeStruct((t, nab), F32)],
        compiler_params=_cparams("parallel"),
        name="inproj",
    )(x, x, x, g1.reshape(1, d), sc, sh, w_main, w_ab, conv_w, alog_row, dtb_row, isdec_row)


def _bdot(a, b):
    return jnp.dot(a.astype(BF16), b.astype(BF16), preferred_element_type=F32)


def _dot_nt(a, b):
    return lax.dot_general(a.astype(BF16), b.astype(BF16), (((1,), (1,)), ((), ())), preferred_element_type=F32)


def _dot_tn(a, b):
    return lax.dot_general(a, b, (((0,), (0,)), ((), ())), preferred_element_type=F32)


def _dn_masks(dirn):
    ii = lax.broadcasted_iota(jnp.int32, (GROUP, GROUP), 0)
    jj = lax.broadcasted_iota(jnp.int32, (GROUP, GROUP), 1)
    same = (ii // CHUNK) == (jj // CHUNK)
    if dirn == 0:
        return same, same & (ii >= jj), same & (ii > jj), same & (ii <= jj)
    return same, same & (ii <= jj), same & (ii < jj), same & (ii >= jj)


def _dn_groups(chains, states, q_s, k_s, v_s, gb_ref, gr_ref, o_refs):
    rng = range(len(chains))
    dh = DN_HEAD_DIM
    masks = {d: _dn_masks(d) for d in sorted({c[1] for c in chains})}
    same = [masks[c[1]][0] for c in chains]

    def pack(bd):
        return bd[0:CHUNK] + bd[CHUNK:2 * CHUNK] + bd[2 * CHUNK:3 * CHUNK] + bd[3 * CHUNK:]

    def unpack(pk, sm):
        return jnp.where(sm, jnp.concatenate([pk, pk, pk, pk], axis=0), 0.0)

    r0s, q, k, v, bcol, gam_col, tot_col, dec = [], [], [], [], [], [], [], []
    for hh, dirn, g in chains:
        r0 = pl.multiple_of(g * GROUP, GROUP)
        cols = slice(hh * dh, (hh + 1) * dh)
        r0s.append(r0)
        q.append(q_s[pl.ds(r0, GROUP), cols])
        k.append(k_s[pl.ds(r0, GROUP), cols])
        v.append(v_s[pl.ds(r0, GROUP), cols])
        cg = 2 * DN_HEADS * dirn + hh
        gcol = gb_ref[pl.ds(r0, GROUP), cg:cg + 1]
        bcol.append(gb_ref[pl.ds(r0, GROUP), cg + DN_HEADS:cg + DN_HEADS + 1])
        grow = gr_ref[hh, dirn, pl.ds(g, 1), :]
        sm, incl, _, incl_t = masks[dirn]
        gc = jnp.sum(jnp.where(incl, grow, 0.0), axis=1, keepdims=True)
        gr = jnp.sum(jnp.where(incl_t, gcol, 0.0), axis=0, keepdims=True)
        gam_col.append(gc)
        tot_col.append(jnp.sum(jnp.where(sm, grow, 0.0), axis=1, keepdims=True))
        dec.append(jnp.where(incl, jnp.exp(jnp.where(incl, gc - gr, 0.0)), 0.0))

    kq = [_dot_nt(jnp.concatenate([k[i], q[i]], axis=0), k[i]) for i in rng]
    mk_bd = [jnp.where(masks[chains[i][1]][2], -(bcol[i] * kq[i][:GROUP] * dec[i]), 0.0) for i in rng]
    attn = [kq[i][GROUP:] * dec[i] for i in rng]

    pi = lax.broadcasted_iota(jnp.int32, (CHUNK, GROUP), 0)
    pj = lax.broadcasted_iota(jnp.int32, (CHUNK, GROUP), 1)
    eye_pk = jnp.where(pi == (pj % CHUNK), 1.0, 0.0)
    mk_pk = [pack(mk_bd[i]) for i in rng]
    p = [eye_pk + mk_pk[i] for i in rng]
    mk_pk = [_bdot(mk_pk[i], mk_bd[i]) for i in rng]
    for _ in range(4):
        mk_b = [unpack(mk_pk[i], same[i]) for i in rng]
        both = [_bdot(jnp.concatenate([p[i], mk_pk[i]], axis=0), mk_b[i]) for i in rng]
        p = [p[i] + both[i][:CHUNK] for i in rng]
        mk_pk = [both[i][CHUNK:] for i in rng]
    p = [p[i] + _bdot(p[i], unpack(mk_pk[i], same[i])) for i in rng]
    t_bd = [unpack(p[i], same[i]) for i in rng]

    eg = [jnp.exp(gam_col[i]) for i in rng]
    rhs = [jnp.concatenate([v[i] * bcol[i], k[i] * (bcol[i] * eg[i])], axis=1) for i in rng]
    uw = [_bdot(t_bd[i], rhs[i]) for i in rng]
    qg = [q[i] * eg[i] for i in rng]
    kend = [k[i] * jnp.exp(tot_col[i] - gam_col[i]) for i in rng]
    gend = [jnp.exp(tot_col[i]) for i in rng]

    nc = GROUP // CHUNK
    v_new_c = [[None] * nc for _ in rng]
    o_state_c = [[None] * nc for _ in rng]
    for step in range(nc):
        cs = [step if chains[i][1] == 0 else nc - 1 - step for i in rng]
        lo = [c * CHUNK for c in cs]
        lhs = [jnp.concatenate([uw[i][lo[i]:lo[i] + CHUNK, dh:], qg[i][lo[i]:lo[i] + CHUNK]], axis=0)
               for i in rng]
        wq = [_bdot(lhs[i], states[i]) for i in rng]
        v_new = [uw[i][lo[i]:lo[i] + CHUNK, :dh] - wq[i][:CHUNK] for i in rng]
        states = [states[i] * gend[i][lo[i]:lo[i] + 1] + _dot_tn(kend[i][lo[i]:lo[i] + CHUNK], v_new[i])
                  for i in rng]
        for i in rng:
            v_new_c[i][cs[i]] = v_new[i]
            o_state_c[i][cs[i]] = wq[i][CHUNK:]
    for i in rng:
        o_all = (jnp.concatenate(o_state_c[i], axis=0)
                 + _bdot(attn[i], jnp.concatenate(v_new_c[i], axis=0)))
        hh = chains[i][0]
        o_refs[chains[i][1]][pl.ds(r0s[i], GROUP), hh * dh:(hh + 1) * dh] = o_all
    return states


def _dn_kernel(q_s, k_s, v_s, gb_ref, gr_ref, s0f_ref, s0b_ref, of_ref, ob_ref, sf_ref, sb_ref, *, seq_len):
    ng = seq_len // GROUP
    nh = DN_HEADS

    def body(m, carry):
        chains = [(hh, d, m if d == 0 else ng - 1 - m) for hh in range(nh) for d in (0, 1)]
        return tuple(_dn_groups(chains, list(carry), q_s, k_s, v_s, gb_ref, gr_ref, (of_ref, ob_ref)))

    init = tuple(r[hh] for hh in range(nh) for r in (s0f_ref, s0b_ref))
    fin = lax.fori_loop(0, ng, body, init)
    for hh in range(nh):
        sf_ref[hh] = fin[2 * hh]
        sb_ref[hh] = fin[2 * hh + 1]


def _deltanet(qkv, gb, grow, s0f, s0b, bsz, seq_len):
    h, dh = DN_HEADS, DN_HEAD_DIM
    ng = seq_len // GROUP
    w = h * dh
    once = pl.Buffered(1)
    col = lambda j: pl.BlockSpec((seq_len, w), lambda b: (b, j), pipeline_mode=once)
    st = pl.BlockSpec((None, h, dh, dh), lambda b: (b, 0, 0, 0))
    return pl.pallas_call(
        functools.partial(_dn_kernel, seq_len=seq_len),
        grid=(bsz,),
        in_specs=[col(0), col(1), col(2),
                  pl.BlockSpec((seq_len, 4 * h), lambda b: (b, 0), pipeline_mode=once),
                  pl.BlockSpec((None, h, 2, ng, GROUP), lambda b: (b, 0, 0, 0, 0)),
                  st, st],
        out_specs=[col(0), col(0), st, st],
        out_shape=[jax.ShapeDtypeStruct((bsz * seq_len, w), F32),
                   jax.ShapeDtypeStruct((bsz * seq_len, w), F32),
                   jax.ShapeDtypeStruct((bsz, h, dh, dh), F32),
                   jax.ShapeDtypeStruct((bsz, h, dh, dh), F32)],
        compiler_params=_cparams("parallel"),
        name="deltanet",
    )(qkv, qkv, qkv, gb, grow, s0f, s0b)


def _dft_table_kernel(tac_ref, tas_ref, tbc_ref, tbs_ref, cp_ref, sp_ref, *, nt1):
    tbc = tbc_ref[...]
    tbs = tbs_ref[...]
    for t1 in range(nt1):
        ac = tac_ref[:, t1:t1 + 1]
        asn = tas_ref[:, t1:t1 + 1]
        cp_ref[:, t1 * LANES:(t1 + 1) * LANES] = (ac * tbc - asn * tbs).astype(BF16)
        sp_ref[:, t1 * LANES:(t1 + 1) * LANES] = (asn * tbc + ac * tbs).astype(BF16)


def _dft_tables(n):
    per = n // LANES
    nt1 = per // 2
    s = np.arange(n, dtype=np.int64)[:, None]
    t1 = np.arange(nt1, dtype=np.int64)[None, :]
    t0 = np.arange(LANES, dtype=np.int64)[None, :]
    ang_a = 2.0 * np.pi * ((s * t1) % per) / per
    ang_b = 2.0 * np.pi * ((s * t0) % n) / n
    pad = ((0, 0), (0, LANES - nt1))
    tac = jnp.asarray(np.pad(np.cos(ang_a), pad), F32)
    tas = jnp.asarray(np.pad(np.sin(ang_a), pad), F32)
    tbc = jnp.asarray(np.cos(ang_b), F32)
    tbs = jnp.asarray(np.sin(ang_b), F32)
    tr = min(n, 256)
    small = pl.BlockSpec((tr, LANES), lambda i: (i, 0))
    big = pl.BlockSpec((tr, n // 2), lambda i: (i, 0))
    return pl.pallas_call(
        functools.partial(_dft_table_kernel, nt1=nt1),
        grid=(n // tr,),
        in_specs=[small, small, small, small],
        out_specs=[big, big],
        out_shape=[jax.ShapeDtypeStruct((n, n // 2), BF16)] * 2,
        compiler_params=_cparams("parallel"),
        name="dft_tables",
    )(tac, tas, tbc, tbs)


MIRROR_HEAD = 2 * SUBLANES


def _fourier_kernel(f_ref, p1_ref, p2_ref, mid_ref, cp_ref, sp_ref, cc_ref, sc_ref, o_ref):
    kstep = pl.program_id(1)
    tk = f_ref.shape[0]
    seq_len = o_ref.shape[0]
    cc = cc_ref[...]
    sc = sc_ref[...]

    def chan(xb, tab):
        return jnp.concatenate([jnp.dot(xb[:, g * LANES:(g + 1) * LANES], tab, preferred_element_type=F32)
                                for g in range(FOURIER_GROUPS)], axis=1)

    xb = f_ref[...].astype(BF16)
    pb = jnp.concatenate([p1_ref[...], p2_ref[...]], axis=0).astype(BF16)
    ri = lax.broadcasted_iota(jnp.int32, (tk, tk + MIRROR_HEAD), 0)
    ci = lax.broadcasted_iota(jnp.int32, (tk, tk + MIRROR_HEAD), 1)
    sel = ((ri >= 1) & (ci == tk - ri)) | ((ri == 0) & (ci == tk) & (kstep > 0))
    xr = jnp.dot(jnp.where(sel, 1.0, 0.0).astype(BF16), pb, preferred_element_type=F32)
    xf = xb.astype(F32)
    zc = chan((xf + xr).astype(BF16), cc).astype(BF16)
    zs = chan((xf - xr).astype(BF16), sc).astype(BF16)
    part = (jnp.dot(cp_ref[...], zc, preferred_element_type=F32)
            - jnp.dot(sp_ref[...], zs, preferred_element_type=F32))

    @pl.when(kstep == 0)
    def _():
        zmid = chan(mid_ref[...].astype(BF16), cc)[0:1, :]
        s_idx = lax.broadcasted_iota(jnp.int32, (seq_len, 1), 0)
        o_ref[...] = part + jnp.where(s_idx % 2 == 0, 1.0, -1.0) * zmid

    @pl.when(kstep > 0)
    def _():
        o_ref[...] += part


def _fourier(f, cp, sp, cc, sc, bsz, seq_len):
    t, fw = f.shape
    tk = min(seq_len // 2, 512)
    nk = seq_len // tk
    hpb = tk // MIRROR_HEAD
    last_head = t // MIRROR_HEAD - 1
    return pl.pallas_call(
        _fourier_kernel,
        grid=(bsz, nk // 2),
        in_specs=[pl.BlockSpec((tk, fw), lambda b, k: (b * nk + k, 0)),
                  pl.BlockSpec((tk, fw), lambda b, k: (b * nk + nk - 1 - k, 0)),
                  pl.BlockSpec((MIRROR_HEAD, fw), lambda b, k: (jnp.minimum((b * nk + nk - k) * hpb, last_head), 0)),
                  pl.BlockSpec((MIRROR_HEAD, fw), lambda b, k: ((b * nk + nk // 2) * hpb, 0)),
                  pl.BlockSpec((seq_len, tk), lambda b, k: (0, k)),
                  pl.BlockSpec((seq_len, tk), lambda b, k: (0, k)),
                  pl.BlockSpec((LANES, LANES), lambda b, k: (0, 0)),
                  pl.BlockSpec((LANES, LANES), lambda b, k: (0, 0))],
        out_specs=pl.BlockSpec((seq_len, fw), lambda b, k: (b, 0), pipeline_mode=pl.Buffered(1)),
        out_shape=jax.ShapeDtypeStruct((bsz * seq_len, fw), F32),
        compiler_params=_cparams("parallel", "arbitrary"),
        name="fourier",
    )(f, f, f, f, cp, sp, cc, sc)


def _outproj_kernel(four_ref, of_ref, ob_ref, z_ref, x_ref, gout_ref, gt1_ref, sc2_ref, sh2_ref, g2_ref,
                    wout_ref, wr_ref, br_ref, cnt0_ref, xn_ref, h2_ref, ti_ref, tg_ref, rk_ref, cnt_ref,
                    carry, *, fw):
    @pl.when(pl.program_id(0) == 0)
    def _():
        carry[...] = cnt0_ref[...]

    o = of_ref[...] + ob_ref[...]
    z = z_ref[...]
    gout = gout_ref[...]
    parts = []
    for hh in range(DN_HEADS):
        sl = slice(hh * DN_HEAD_DIM, (hh + 1) * DN_HEAD_DIM)
        oh = o[:, sl]
        zh = z[:, sl]
        oh = oh * lax.rsqrt(jnp.mean(oh * oh, axis=-1, keepdims=True) + EPS) * gout
        parts.append(oh * (zh * _sigmoid(zh)))
    gated = jnp.concatenate(parts, axis=1).astype(BF16)
    mix = (jnp.dot(four_ref[...].astype(BF16), wout_ref[:fw, :], preferred_element_type=F32)
           + jnp.dot(gated, wout_ref[fw:, :], preferred_element_type=F32))
    xn = x_ref[...] + gt1_ref[...] * mix
    xn_ref[...] = xn
    inv = lax.rsqrt(jnp.mean(xn * xn, axis=-1, keepdims=True) + EPS)
    h2 = (xn * inv * g2_ref[...]) * (1.0 + sc2_ref[...]) + sh2_ref[...]
    tm = h2.shape[0]
    for s in range(h2.shape[1] // LANES):
        h2_ref[pl.ds(s, tm, stride=SUBLANES), :] = h2[:, s * LANES:(s + 1) * LANES]
    wr = wr_ref[...]
    wr_hi = wr.astype(BF16)
    wr_lo = (wr - wr_hi.astype(F32)).astype(BF16)
    h2_hi = h2.astype(BF16)
    h2_lo = (h2 - h2_hi.astype(F32)).astype(BF16)
    logits = (jnp.dot(h2_hi, wr_hi, preferred_element_type=F32) + jnp.dot(h2_hi, wr_lo, preferred_element_type=F32)
              + jnp.dot(h2_lo, wr_hi, preferred_element_type=F32)) + br_ref[...]
    lane = lax.broadcasted_iota(jnp.int32, logits.shape, 1)
    neg = jnp.float32(-jnp.inf)
    logits = jnp.where(lane < N_EXPERTS, logits, neg)
    vals, idxs = [], []
    for _ in range(TOP_K):
        m = jnp.max(logits, axis=-1, keepdims=True)
        idx = jnp.min(jnp.where(logits == m, lane, LANES), axis=-1, keepdims=True)
        vals.append(m)
        idxs.append(idx)
        logits = jnp.where(lane == idx, neg, logits)
    ex = [jnp.exp(vv - vals[0]) for vv in vals]
    den = ex[0] + ex[1] + ex[2] + ex[3]
    onehot = jnp.zeros(logits.shape, F32)
    for j in range(TOP_K):
        onehot = onehot + jnp.where(lane == idxs[j], 1.0, 0.0)
    ri = lax.broadcasted_iota(jnp.int32, (tm, tm), 0)
    ci = lax.broadcasted_iota(jnp.int32, (tm, tm), 1)
    tri = jnp.where(ri > ci, 1.0, 0.0).astype(BF16)
    base = jnp.dot(tri, onehot.astype(BF16), preferred_element_type=F32) + carry[...]
    carry[...] = carry[...] + jnp.sum(onehot, axis=0, keepdims=True)
    cnt_ref[...] = carry[...]
    kl = lax.broadcasted_iota(jnp.int32, (tm, TOP_K), 1)
    ti = jnp.zeros((tm, TOP_K), jnp.int32)
    tg = jnp.zeros((tm, TOP_K), F32)
    rk = jnp.zeros((tm, TOP_K), jnp.int32)
    for j in range(TOP_K):
        rank_j = jnp.sum(jnp.where(lane == idxs[j], base, 0.0), axis=-1, keepdims=True).astype(jnp.int32)
        ti = jnp.where(kl == j, idxs[j], ti)
        tg = jnp.where(kl == j, ex[j] / den, tg)
        rk = jnp.where(kl == j, rank_j, rk)
    ti_ref[...] = ti
    tg_ref[...] = tg
    rk_ref[...] = rk


def _outproj(four, o_f, o_b, z, x, g_out, gt1, sc2, sh2, g2, w_out, w_r, b_r, cnt0, rows_per_group, tm=256):
    t, d = x.shape
    fw = four.shape[1]
    ng = gt1.shape[0]
    tpg = rows_per_group // tm
    grp = lambda i: (jnp.minimum(i // tpg, ng - 1), 0, 0)
    const = lambda i: (0, 0)
    row = lambda w: pl.BlockSpec((tm, w), lambda i: (i, 0))
    gspec = pl.BlockSpec((None, 1, d), grp)
    return pl.pallas_call(
        functools.partial(_outproj_kernel, fw=fw),
        grid=(t // tm,),
        in_specs=[row(fw), row(z.shape[1]), row(z.shape[1]), row(z.shape[1]), row(d),
                  pl.BlockSpec((1, DN_HEAD_DIM), const), gspec, gspec, gspec,
                  pl.BlockSpec((1, d), const),
                  pl.BlockSpec(w_out.shape, const),
                  pl.BlockSpec(w_r.shape, const),
                  pl.BlockSpec((1, LANES), const),
                  pl.BlockSpec((1, LANES), const)],
        out_specs=[row(d),
                   pl.BlockSpec((tm * SUBLANES, LANES), lambda i: (i, 0)),
                   row(TOP_K), row(TOP_K), row(TOP_K),
                   pl.BlockSpec((1, LANES), const)],
        out_shape=[jax.ShapeDtypeStruct((t, d), F32),
                   jax.ShapeDtypeStruct((t * d // LANES, LANES), F32),
                   jax.ShapeDtypeStruct((t, TOP_K), jnp.int32),
                   jax.ShapeDtypeStruct((t, TOP_K), F32),
                   jax.ShapeDtypeStruct((t, TOP_K), jnp.int32),
                   jax.ShapeDtypeStruct((1, LANES), F32)],
        scratch_shapes=[pltpu.VMEM((1, LANES), F32)],
        compiler_params=_cparams("arbitrary"),
        name="outproj",
    )(four, o_f, o_b, z, x, g_out.reshape(1, DN_HEAD_DIM), gt1, sc2, sh2, g2.reshape(1, d), w_out, w_r, b_r, cnt0)


TOK_UNROLL = 4


def _row_copy(src, src_row, dst, dst_row, sem):
    return pltpu.make_async_copy(
        src.at[pl.ds(pl.multiple_of(src_row * SUBLANES, SUBLANES), SUBLANES), :],
        dst.at[pl.ds(pl.multiple_of(dst_row * SUBLANES, SUBLANES), SUBLANES), :], sem)


def _dispatch_kernel(cnt_ref, ps_ref, pd_ref, nu_ref, *refs, tm, steps, n_blocks):
    ns = len(steps)
    dest_refs, h_refs = refs[0:2 * ns:2], refs[1:2 * ns:2]
    xs_hbm, zero_s, sem, sem_pad = refs[2 * ns:]
    i = pl.program_id(0)
    blk_rows = MOE_BLOCK * SUBLANES

    def tail_copy(b):
        return pltpu.make_async_copy(zero_s, xs_hbm.at[pl.ds(pl.multiple_of(b * blk_rows, blk_rows), blk_rows), :],
                                     sem_pad)

    @pl.when(i == 0)
    def _():
        zero_s[...] = jnp.zeros(zero_s.shape, F32)

        def per_expert(e, tot):
            def one(r, c):
                _row_copy(zero_s, 0, xs_hbm, ps_ref[e] + r, sem_pad).start()
                return c

            lax.fori_loop(cnt_ref[e], pd_ref[e], one, 0)
            return tot + pd_ref[e] - cnt_ref[e]

        n_pad = lax.fori_loop(0, N_EXPERTS, per_expert, 0)

        def start_tail(b, c):
            tail_copy(b).start()
            return c

        lax.fori_loop(nu_ref[0], n_blocks, start_tail, 0)

        def wait_row(r, c):
            _row_copy(zero_s, 0, xs_hbm, 0, sem_pad).wait()
            return c

        lax.fori_loop(0, n_pad, wait_row, 0)

        def wait_tail(b, c):
            tail_copy(b).wait()
            return c

        lax.fori_loop(nu_ref[0], n_blocks, wait_tail, 0)

    first = 0
    for dest_ref, h_ref, nst in zip(dest_refs, h_refs, steps):
        @pl.when((i >= first) & (i < first + nst))
        def _(dest_ref=dest_ref, h_ref=h_ref):
            def issue(blk, carry):
                for tt in range(TOK_UNROLL):
                    t = blk * TOK_UNROLL + tt
                    for j in range(TOP_K):
                        _row_copy(h_ref, t, xs_hbm, dest_ref[0, 0, t * TOP_K + j], sem).start(priority=j % 2)
                return carry

            lax.fori_loop(0, tm // TOK_UNROLL, issue, 0)
            for _ in range(TOP_K):
                pltpu.make_async_copy(h_ref, xs_hbm.at[pl.ds(0, tm * SUBLANES), :], sem).wait()

        first += nst


def _dispatch(streams, counts, pstart, padded, n_used, n_blocks, tm=512):
    seg = streams[0][0].shape[0] // streams[0][1].shape[0]
    steps = [dest.shape[0] // tm for _, dest in streams]
    in_specs, args, first = [], [], 0
    for (h_rt, dest), nst in zip(streams, steps):
        idx = lambda i, *_, first=first, nst=nst: (jnp.clip(i - first, 0, nst - 1), 0)
        idx3 = lambda i, *_, idx=idx: idx(i) + (0,)
        in_specs += [pl.BlockSpec((1, 1, tm * TOP_K), idx3, memory_space=pltpu.SMEM),
                     pl.BlockSpec((tm * seg, LANES), idx)]
        args += [dest.reshape(nst, 1, tm * TOP_K), h_rt]
        first += nst
    grid_spec = pltpu.PrefetchScalarGridSpec(
        num_scalar_prefetch=4,
        grid=(first,),
        in_specs=in_specs,
        out_specs=pl.BlockSpec(memory_space=pltpu.HBM),
        scratch_shapes=[pltpu.VMEM((MOE_BLOCK * seg, LANES), F32),
                        pltpu.SemaphoreType.DMA(()), pltpu.SemaphoreType.DMA(())],
    )
    return pl.pallas_call(
        functools.partial(_dispatch_kernel, tm=tm, steps=tuple(steps), n_blocks=n_blocks),
        grid_spec=grid_spec,
        out_shape=jax.ShapeDtypeStruct((n_blocks * MOE_BLOCK * seg, LANES), F32),
        compiler_params=_cparams("arbitrary"),
        name="dispatch",
    )(counts, pstart, padded, n_used, *args)


def _expert_kernel(be_ref, nu_ref, xs_ref, wg_ref, bg_ref, wu_ref, bu_ref, wd_ref, bd_ref, y_ref,
                   x2d, wg_b, wu_b, wd_b):
    i = pl.program_id(0)
    rows = MOE_BLOCK
    nseg = x2d.shape[1] // LANES

    @pl.when(i < nu_ref[0])
    def _():
        @pl.when((i == 0) | (be_ref[i] != be_ref[jnp.maximum(i - 1, 0)]))
        def _():
            wg_b[...] = wg_ref[...].astype(BF16)
            wu_b[...] = wu_ref[...].astype(BF16)
            wd_b[...] = wd_ref[...].astype(BF16)

        for s in range(nseg):
            x2d[:, s * LANES:(s + 1) * LANES] = xs_ref[pl.ds(s, rows, stride=SUBLANES), :].astype(BF16)
        x = x2d[...]
        a = jnp.minimum(jnp.dot(x, wg_b[...], preferred_element_type=F32) + bg_ref[...], SWIGLU_LIMIT)
        u = jnp.clip(jnp.dot(x, wu_b[...], preferred_element_type=F32) + bu_ref[...], -SWIGLU_LIMIT, SWIGLU_LIMIT)
        act = (a * _sigmoid(SWIGLU_ALPHA * a) * (u + 1.0)).astype(BF16)
        y = jnp.dot(act, wd_b[...], preferred_element_type=F32) + bd_ref[...]
        for s in range(nseg):
            y_ref[pl.ds(s, rows, stride=SUBLANES), :] = y[:, s * LANES:(s + 1) * LANES]

    @pl.when(i >= nu_ref[0])
    def _():
        y_ref[...] = jnp.zeros(y_ref.shape, F32)


def _experts(xs, block_e, n_used, layer, w_gate, b_gate, w_up, b_up, w_down, b_down):
    n_blocks = block_e.shape[0]
    depth, ne, d, ff = w_gate.shape
    seg = d // LANES
    blk = lambda i, nu: jnp.minimum(i, jnp.maximum(nu[0] - 1, 0))
    wspec = lambda shp: pl.BlockSpec((None, None) + shp, lambda i, be, nu: (layer, be[blk(i, nu)], 0, 0))
    grid_spec = pltpu.PrefetchScalarGridSpec(
        num_scalar_prefetch=2,
        grid=(n_blocks,),
        in_specs=[pl.BlockSpec((MOE_BLOCK * seg, LANES), lambda i, be, nu: (blk(i, nu), 0)),
                  wspec((d, ff)), wspec((1, ff)), wspec((d, ff)), wspec((1, ff)), wspec((ff, d)), wspec((1, d))],
        out_specs=pl.BlockSpec((MOE_BLOCK * seg, LANES), lambda i, be, nu: (i, 0)),
        scratch_shapes=[pltpu.VMEM((MOE_BLOCK, d), BF16), pltpu.VMEM((d, ff), BF16), pltpu.VMEM((d, ff), BF16),
                        pltpu.VMEM((ff, d), BF16)],
    )
    return pl.pallas_call(
        _expert_kernel,
        grid_spec=grid_spec,
        out_shape=jax.ShapeDtypeStruct((n_blocks * MOE_BLOCK * seg, LANES), F32),
        compiler_params=_cparams("arbitrary"),
        name="experts",
    )(block_e, n_used, xs, w_gate, b_gate.reshape(depth, ne, 1, ff), w_up, b_up.reshape(depth, ne, 1, ff),
      w_down, b_down.reshape(depth, ne, 1, d))


def _combine_kernel(dest_ref, dnext_ref, y_hbm, x_ref, tg_ref, gt_ref, gf_ref, o_ref, buf0, buf1, sem0, sem1, *,
                    tm, final_norm):
    n = tm * TOP_K
    i = pl.program_id(0)
    last = pl.num_programs(0) - 1

    def gather(d_ref, buf, sem):
        def issue(blk, carry):
            for tt in range(TOK_UNROLL):
                t = blk * TOK_UNROLL + tt
                for j in range(TOP_K):
                    _row_copy(y_hbm, d_ref[0, 0, t * TOP_K + j], buf, j * tm + t, sem).start(priority=j % 2)
            return carry

        lax.fori_loop(0, tm // TOK_UNROLL, issue, 0)

    def reduce(buf, sem):
        pltpu.make_async_copy(y_hbm.at[pl.ds(0, n * SUBLANES), :], buf, sem).wait()
        gt = gt_ref[...]
        tg = tg_ref[...]
        d = o_ref.shape[1]
        ssq = jnp.zeros((tm, 1), F32)
        for s in range(d // LANES):
            acc = tg[:, 0:1] * buf[pl.ds(s, tm, stride=SUBLANES), :]
            for j in range(1, TOP_K):
                acc = acc + tg[:, j:j + 1] * buf[pl.ds(j * tm * SUBLANES + s, tm, stride=SUBLANES), :]
            sl = slice(s * LANES, (s + 1) * LANES)
            o = x_ref[:, sl] + gt[:, sl] * acc
            o_ref[:, sl] = o
            if final_norm:
                ssq = ssq + jnp.sum(o * o, axis=-1, keepdims=True)
        if final_norm:
            inv = lax.rsqrt(ssq / d + EPS)
            o_ref[...] = o_ref[...] * inv * gf_ref[...]

    @pl.when(i == 0)
    def _():
        gather(dest_ref, buf0, sem0)

    for par, (cur, nxt) in enumerate((((buf0, sem0), (buf1, sem1)), ((buf1, sem1), (buf0, sem0)))):
        @pl.when(i % 2 == par)
        def _(cur=cur, nxt=nxt):
            @pl.when(i < last)
            def _():
                gather(dnext_ref, *nxt)

            reduce(*cur)


def _combine(x, y_rt, dest, top_g, gt2, g_final, final_norm, rows_per_group, tm=256):
    t, d = x.shape
    ng = gt2.shape[0]
    nt = t // tm
    tpg = rows_per_group // tm
    dest3 = dest.reshape(nt, 1, tm * TOP_K)
    buf = pltpu.VMEM((tm * TOP_K * SUBLANES, LANES), F32)
    return pl.pallas_call(
        functools.partial(_combine_kernel, tm=tm, final_norm=final_norm),
        grid=(nt,),
        in_specs=[pl.BlockSpec((1, 1, tm * TOP_K), lambda i: (i, 0, 0), memory_space=pltpu.SMEM),
                  pl.BlockSpec((1, 1, tm * TOP_K), lambda i: (jnp.minimum(i + 1, nt - 1), 0, 0),
                               memory_space=pltpu.SMEM),
                  pl.BlockSpec(memory_space=pltpu.HBM),
                  pl.BlockSpec((tm, d), lambda i: (i, 0)),
                  pl.BlockSpec((tm, TOP_K), lambda i: (i, 0)),
                  pl.BlockSpec((None, 1, d), lambda i: (jnp.minimum(i // tpg, ng - 1), 0, 0)),
                  pl.BlockSpec((1, d), lambda i: (0, 0))],
        out_specs=pl.BlockSpec((tm, d), lambda i: (i, 0)),
        out_shape=jax.ShapeDtypeStruct((t, d), F32),
        scratch_shapes=[buf, buf, pltpu.SemaphoreType.DMA(()), pltpu.SemaphoreType.DMA(())],
        compiler_params=_cparams("arbitrary"),
        name="combine",
    )(dest3, dest3, y_rt, x, top_g, gt2, g_final.reshape(1, d))


def _slot_tables(cnt_row, n_tok):
    n_blocks = -(-n_tok * TOP_K // MOE_BLOCK) + N_EXPERTS
    counts = cnt_row[0, :N_EXPERTS].astype(jnp.int32)
    padded = (counts + MOE_BLOCK - 1) // MOE_BLOCK * MOE_BLOCK
    padded_end = jnp.cumsum(padded)
    pstart = padded_end - padded
    blk_start = jnp.arange(n_blocks, dtype=jnp.int32) * MOE_BLOCK
    block_e = jnp.minimum(jnp.sum(padded_end[None, :] <= blk_start[:, None], axis=1), N_EXPERTS - 1).astype(jnp.int32)
    n_used = (padded_end[-1:] // MOE_BLOCK).astype(jnp.int32)
    return counts, pstart.astype(jnp.int32), padded.astype(jnp.int32), block_e, n_used, n_blocks


def _slot_of(top_i, rank, pstart):
    e = jnp.arange(N_EXPERTS, dtype=jnp.int32)
    return rank + jnp.sum(jnp.where(top_i[..., None] == e, pstart, 0), axis=-1)


def _decay_rows(gb, bsz, seq_len):
    g5 = gb.reshape(bsz, seq_len, 2, 2, DN_HEADS)
    return g5[:, :, :, 0, :].transpose(0, 3, 2, 1).reshape(bsz, DN_HEADS, 2, seq_len // GROUP, GROUP)


def kernel(x, c, ctx, c_ctx, w_mod, b_mod, g_norm1, w_in, conv_w, a_log, dt_bias, g_out_norm, w_out, g_norm2,
           w_router, b_router, w_gate, b_gate, w_up, b_up, w_down, b_down, g_final):
    bsz, seq, d = x.shape
    clen = ctx.shape[1]
    depth = w_mod.shape[0]
    fw = FOURIER_GROUPS * LANES
    nmain = fw + 4 * DN_HEADS * DN_HEAD_DIM
    nab = 4 * DN_HEADS

    xt = x.reshape(bsz * seq, d)
    ct = ctx.reshape(bsz * clen, d)
    c_rows = jnp.concatenate([c, c_ctx[None, :], jnp.zeros((2 * SUBLANES - bsz - 1, d), F32)], axis=0)

    cp_x, sp_x = _dft_tables(seq)
    cp_c, sp_c = _dft_tables(clen)
    ch = np.arange(LANES, dtype=np.int64)
    ang = 2.0 * np.pi * ((ch[:, None] * ch[None, :]) % LANES) / LANES

    def chan(tab, n):
        return jnp.asarray(tab / math.sqrt(n * LANES), F32).astype(BF16)

    lanepad = lambda v: jnp.pad(v, ((0, 0), (0, LANES - v.shape[1])))
    zeros4 = jnp.zeros((DN_HEADS,), F32)
    isdec_row = lanepad(jnp.tile(jnp.concatenate([jnp.ones((DN_HEADS,), F32), zeros4]), 2)[None, :])

    for l in range(depth):
        update_ctx = l < depth - 1
        mod = _mod(c_rows, w_mod[l], b_mod[l]).reshape(2 * SUBLANES, 6, 1, d)
        mx = lambda j: mod[:bsz, j]
        mc = lambda j: mod[bsz:bsz + 1, j]

        w_main = w_in[l][:, :nmain].astype(BF16)
        w_ab = lanepad(w_in[l][:, nmain:]).astype(BF16)
        alog_row = lanepad(jnp.concatenate([a_log[l][0], zeros4, a_log[l][1], zeros4])[None, :])
        dtb_row = lanepad(jnp.concatenate([dt_bias[l][0], zeros4, dt_bias[l][1], zeros4])[None, :])
        ip = lambda tok, sh, sc, rpg, slen: _inproj(tok, g_norm1[l], sc, sh, w_main, w_ab, conv_w[l], alog_row,
                                                     dtb_row, isdec_row, rpg, slen)
        f_x, qkv_x, z_x, gb_x = ip(xt, mx(0), mx(1), seq, seq)
        f_c, qkv_c, z_c, gb_c = ip(ct, mc(0), mc(1), bsz * clen, clen)

        s_zero = jnp.zeros((bsz, DN_HEADS, DN_HEAD_DIM, DN_HEAD_DIM), F32)
        ocf, ocb, s_f, s_b = _deltanet(qkv_c, gb_c, _decay_rows(gb_c, bsz, clen), s_zero, s_zero, bsz, clen)
        oxf, oxb, _, _ = _deltanet(qkv_x, gb_x, _decay_rows(gb_x, bsz, seq), s_f, s_b, bsz, seq)

        w_out_b = w_out[l].astype(BF16)
        w_r = lanepad(w_router[l])
        b_r = lanepad(b_router[l][None, :])
        four_x = _fourier(f_x, cp_x, sp_x, chan(np.cos(ang), seq), chan(np.sin(ang), seq), bsz, seq)
        cnt0 = jnp.zeros((1, LANES), F32)
        xn_x, h2_x, ti_x, tg_x, rk_x, cnt = _outproj(four_x, oxf, oxb, z_x, xt, g_out_norm[l], mx(2), mx(4), mx(3),
                                                      g_norm2[l], w_out_b, w_r, b_r, cnt0, seq)
        n_tok = bsz * seq
        if update_ctx:
            four_c = _fourier(f_c, cp_c, sp_c, chan(np.cos(ang), clen), chan(np.sin(ang), clen), bsz, clen)
            xn_c, h2_c, ti_c, tg_c, rk_c, cnt = _outproj(four_c, ocf, ocb, z_c, ct, g_out_norm[l], mc(2), mc(4),
                                                          mc(3), g_norm2[l], w_out_b, w_r, b_r, cnt, bsz * clen)
            n_tok += bsz * clen

        counts, pstart, padded, block_e, n_used, n_blocks = _slot_tables(cnt, n_tok)
        dest_x = _slot_of(ti_x, rk_x, pstart)
        streams = [(h2_x, dest_x)]
        if update_ctx:
            dest_c = _slot_of(ti_c, rk_c, pstart)
            streams.append((h2_c, dest_c))
        xs = _dispatch(streams, counts, pstart, padded, n_used, n_blocks)
        y_rt = _experts(xs, block_e, n_used, l, w_gate, b_gate, w_up, b_up, w_down, b_down)
        xt = _combine(xn_x, y_rt, dest_x, tg_x, mx(5), g_final, l == depth - 1, seq)
        if update_ctx:
            ct = _combine(xn_c, y_rt, dest_c, tg_c, mc(5), g_final, False, bsz * clen)

    return xt.reshape(bsz, seq, d)
```

```python
import functools
import math

import numpy as np
import jax
import jax.numpy as jnp
from jax import lax
from jax.experimental import pallas as pl
from jax.experimental.pallas import tpu as pltpu

FOURIER_GROUPS = 4
DN_HEADS = 4
DN_HEAD_DIM = 128
CONV_K = 5
N_EXPERTS = 32
TOP_K = 4
SWIGLU_LIMIT = 7.0
SWIGLU_ALPHA = 1.702
EPS = 1e-6

LANES = 128
SUBLANES = 8
V7X_VMEM_BYTES = 64 * 1024 * 1024
VMEM_LIMIT = V7X_VMEM_BYTES - 8 * 1024 * 1024
CHUNK = 64
GROUP = 4 * CHUNK
MOE_BLOCK = 256

F32 = jnp.float32
BF16 = jnp.bfloat16


def _cparams(*sem):
    return pltpu.CompilerParams(dimension_semantics=sem, vmem_limit_bytes=VMEM_LIMIT)


def _sigmoid(x):
    return 1.0 / (1.0 + jnp.exp(-x))


def _mod_kernel(c_ref, w_ref, b_ref, o_ref):
    c = c_ref[...]
    a = c * _sigmoid(c)
    o_ref[...] = jnp.dot(a, w_ref[...], preferred_element_type=F32) + b_ref[...]


def _mod(c_rows, w_mod, b_mod):
    r, d = c_rows.shape
    n = w_mod.shape[1]
    tn = n // 4
    return pl.pallas_call(
        _mod_kernel,
        grid=(n // tn,),
        in_specs=[pl.BlockSpec((r, d), lambda j: (0, 0)),
                  pl.BlockSpec((d, tn), lambda j: (0, j)),
                  pl.BlockSpec((1, tn), lambda j: (0, j))],
        out_specs=pl.BlockSpec((r, tn), lambda j: (0, j)),
        out_shape=jax.ShapeDtypeStruct((r, n), F32),
        compiler_params=_cparams("parallel"),
        name="mod",
    )(c_rows, w_mod, b_mod.reshape(1, n))


def _inproj_kernel(x_ref, xp_ref, xn_ref, g_ref, sc_ref, sh_ref, w_ref, wab_ref, cw_ref, alog_ref, dtb_ref,
                   isdec_ref, f_ref, qkv_ref, z_ref, gb_ref, *, fw, qw, seq_len):
    i = pl.program_id(0)
    tm = x_ref.shape[0]
    halo = SUBLANES

    def norm_mod(x):
        inv = lax.rsqrt(jnp.mean(x * x, axis=-1, keepdims=True) + EPS)
        return (x * inv * g_ref[...]) * (1.0 + sc_ref[...]) + sh_ref[...]

    h = norm_mod(x_ref[...])
    hb = h.astype(BF16)
    at_start = (i * tm) % seq_len == 0
    at_end = ((i + 1) * tm) % seq_len == 0
    hp = jnp.where(at_start, 0.0, norm_mod(xp_ref[...]))
    hn = jnp.where(at_end, 0.0, norm_mod(xn_ref[...]))
    hext = jnp.concatenate([hp, h, hn], axis=0).astype(BF16)
    f_ref[...] = jnp.dot(hb, w_ref[:, :fw], preferred_element_type=F32)
    n = tm + 2 * halo
    cwid = 2 * DN_HEAD_DIM
    for c in range(3 * qw // cwid):
        lo = c * cwid
        pre = jnp.dot(hext, w_ref[:, fw + lo:fw + lo + cwid], preferred_element_type=F32)
        acc = None
        for j in range(CONV_K):
            off = halo - CONV_K // 2 + j
            term = cw_ref[j:j + 1, lo:lo + cwid] * pltpu.roll(pre, n - off, 0)[:tm]
            acc = term if acc is None else acc + term
        y = acc * _sigmoid(acc)
        if lo < 2 * qw:
            scale = DN_HEAD_DIM ** -0.5 if lo < qw else 1.0
            segs = []
            for hh in range(cwid // DN_HEAD_DIM):
                ys = y[:, hh * DN_HEAD_DIM:(hh + 1) * DN_HEAD_DIM]
                segs.append(ys * (lax.rsqrt(jnp.sum(ys * ys, axis=-1, keepdims=True) + EPS) * scale))
            y = jnp.concatenate(segs, axis=1)
        qkv_ref[:, lo:lo + cwid] = y
    lo = fw + 3 * qw
    z_ref[...] = jnp.dot(hb, w_ref[:, lo:lo + qw], preferred_element_type=F32)
    ab = jnp.dot(hb, wab_ref[...], preferred_element_type=F32)
    pre = ab + dtb_ref[...]
    softplus = jnp.maximum(pre, 0.0) + jnp.log(1.0 + jnp.exp(-jnp.abs(pre)))
    dec = -jnp.exp(alog_ref[...]) * softplus
    gb = jnp.where(isdec_ref[...] > 0.0, dec, _sigmoid(ab))
    gb_ref[...] = gb[:, :gb_ref.shape[1]]


def _inproj(x, g1, sc, sh, w_main, w_ab, conv_w, alog_row, dtb_row, isdec_row, rows_per_group, seq_len):
    t, d = x.shape
    ng = sc.shape[0]
    fw = FOURIER_GROUPS * LANES
    qw = DN_HEADS * DN_HEAD_DIM
    nab = 4 * DN_HEADS
    tm = min(512, seq_len)
    hpt = tm // SUBLANES
    tpg = rows_per_group // tm
    grp = lambda i: (jnp.minimum(i // tpg, ng - 1), 0, 0)
    const = lambda i: (0, 0)
    return pl.pallas_call(
        functools.partial(_inproj_kernel, fw=fw, qw=qw, seq_len=seq_len),
        grid=(t // tm,),
        in_specs=[pl.BlockSpec((tm, d), lambda i: (i, 0)),
                  pl.BlockSpec((SUBLANES, d), lambda i: (jnp.maximum(i * hpt - 1, 0), 0)),
                  pl.BlockSpec((SUBLANES, d), lambda i: (jnp.minimum((i + 1) * hpt, t // SUBLANES - 1), 0)),
                  pl.BlockSpec((1, d), const),
                  pl.BlockSpec((None, 1, d), grp),
                  pl.BlockSpec((None, 1, d), grp),
                  pl.BlockSpec(w_main.shape, const),
                  pl.BlockSpec(w_ab.shape, const),
                  pl.BlockSpec(conv_w.shape, const),
                  pl.BlockSpec((1, LANES), const),
                  pl.BlockSpec((1, LANES), const),
                  pl.BlockSpec((1, LANES), const)],
        out_specs=[pl.BlockSpec((tm, fw), lambda i: (i, 0)),
                   pl.BlockSpec((tm, 3 * qw), lambda i: (i, 0)),
                   pl.BlockSpec((tm, qw), lambda i: (i, 0)),
                   pl.BlockSpec((tm, nab), lambda i: (i, 0))],
        out_shape=[jax.ShapeDtypeStruct((t, fw), F32),
                   jax.ShapeDtypeStruct((t, 3 * qw), F32),
                   jax.ShapeDtypeStruct((t, qw), F32),
                   jax.ShapeDtypeStruct((t, nab), F32)],
        compiler_params=_cparams("parallel"),
        name="inproj",
    )(x, x, x, g1.reshape(1, d), sc, sh, w_main, w_ab, conv_w, alog_row, dtb_row, isdec_row)


def _bdot(a, b):
    return jnp.dot(a.astype(BF16), b.astype(BF16), preferred_element_type=F32)


def _dot_nt(a, b):
    return lax.dot_general(a.astype(BF16), b.astype(BF16), (((1,), (1,)), ((), ())), preferred_element_type=F32)


def _dot_tn(a, b):
    return lax.dot_general(a, b, (((0,), (0,)), ((), ())), preferred_element_type=F32)


def _dn_masks(dirn):
    ii = lax.broadcasted_iota(jnp.int32, (GROUP, GROUP), 0)
    jj = lax.broadcasted_iota(jnp.int32, (GROUP, GROUP), 1)
    same = (ii // CHUNK) == (jj // CHUNK)
    if dirn == 0:
        return same, same & (ii >= jj), same & (ii > jj), same & (ii <= jj)
    return same, same & (ii <= jj), same & (ii < jj), same & (ii >= jj)


def _dn_groups(chains, states, q_s, k_s, v_s, gb_ref, gr_ref, o_refs):
    rng = range(len(chains))
    dh = DN_HEAD_DIM
    masks = {d: _dn_masks(d) for d in sorted({c[1] for c in chains})}
    same = [masks[c[1]][0] for c in chains]

    def pack(bd):
        return bd[0:CHUNK] + bd[CHUNK:2 * CHUNK] + bd[2 * CHUNK:3 * CHUNK] + bd[3 * CHUNK:]

    def unpack(pk, sm):
        return jnp.where(sm, jnp.concatenate([pk, pk, pk, pk], axis=0), 0.0)

    r0s, q, k, v, bcol, gam_col, tot_col, dec = [], [], [], [], [], [], [], []
    for hh, dirn, g in chains:
        r0 = pl.multiple_of(g * GROUP, GROUP)
        cols = slice(hh * dh, (hh + 1) * dh)
        r0s.append(r0)
        q.append(q_s[pl.ds(r0, GROUP), cols])
        k.append(k_s[pl.ds(r0, GROUP), cols])
        v.append(v_s[pl.ds(r0, GROUP), cols])
        cg = 2 * DN_HEADS * dirn + hh
        gcol = gb_ref[pl.ds(r0, GROUP), cg:cg + 1]
        bcol.append(gb_ref[pl.ds(r0, GROUP), cg + DN_HEADS:cg + DN_HEADS + 1])
        grow = gr_ref[hh, dirn, pl.ds(g, 1), :]
        sm, incl, _, incl_t = masks[dirn]
        gc = jnp.sum(jnp.where(incl, grow, 0.0), axis=1, keepdims=True)
        gr = jnp.sum(jnp.where(incl_t, gcol, 0.0), axis=0, keepdims=True)
        gam_col.append(gc)
        tot_col.append(jnp.sum(jnp.where(sm, grow, 0.0), axis=1, keepdims=True))
        dec.append(jnp.where(incl, jnp.exp(jnp.where(incl, gc - gr, 0.0)), 0.0))

    kq = [_dot_nt(jnp.concatenate([k[i], q[i]], axis=0), k[i]) for i in rng]
    mk_bd = [jnp.where(masks[chains[i][1]][2], -(bcol[i] * kq[i][:GROUP] * dec[i]), 0.0) for i in rng]
    attn = [kq[i][GROUP:] * dec[i] for i in rng]

    pi = lax.broadcasted_iota(jnp.int32, (CHUNK, GROUP), 0)
    pj = lax.broadcasted_iota(jnp.int32, (CHUNK, GROUP), 1)
    eye_pk = jnp.where(pi == (pj % CHUNK), 1.0, 0.0)
    mk_pk = [pack(mk_bd[i]) for i in rng]
    p = [eye_pk + mk_pk[i] for i in rng]
    mk_pk = [_bdot(mk_pk[i], mk_bd[i]) for i in rng]
    for _ in range(4):
        mk_b = [unpack(mk_pk[i], same[i]) for i in rng]
        both = [_bdot(jnp.concatenate([p[i], mk_pk[i]], axis=0), mk_b[i]) for i in rng]
        p = [p[i] + both[i][:CHUNK] for i in rng]
        mk_pk = [both[i][CHUNK:] for i in rng]
    p = [p[i] + _bdot(p[i], unpack(mk_pk[i], same[i])) for i in rng]
    t_bd = [unpack(p[i], same[i]) for i in rng]

    eg = [jnp.exp(gam_col[i]) for i in rng]
    rhs = [jnp.concatenate([v[i] * bcol[i], k[i] * (bcol[i] * eg[i])], axis=1) for i in rng]
    uw = [_bdot(t_bd[i], rhs[i]) for i in rng]
    qg = [q[i] * eg[i] for i in rng]
    kend = [k[i] * jnp.exp(tot_col[i] - gam_col[i]) for i in rng]
    gend = [jnp.exp(tot_col[i]) for i in rng]

    nc = GROUP // CHUNK
    v_new_c = [[None] * nc for _ in rng]
    o_state_c = [[None] * nc for _ in rng]
    for step in range(nc):
        cs = [step if chains[i][1] == 0 else nc - 1 - step for i in rng]
        lo = [c * CHUNK for c in cs]
        lhs = [jnp.concatenate([uw[i][lo[i]:lo[i] + CHUNK, dh:], qg[i][lo[i]:lo[i] + CHUNK]], axis=0)
               for i in rng]
        wq = [_bdot(lhs[i], states[i]) for i in rng]
        v_new = [uw[i][lo[i]:lo[i] + CHUNK, :dh] - wq[i][:CHUNK] for i in rng]
        states = [states[i] * gend[i][lo[i]:lo[i] + 1] + _dot_tn(kend[i][lo[i]:lo[i] + CHUNK], v_new[i])
                  for i in rng]
        for i in rng:
            v_new_c[i][cs[i]] = v_new[i]
            o_state_c[i][cs[i]] = wq[i][CHUNK:]
    for i in rng:
        o_all = (jnp.concatenate(o_state_c[i], axis=0)
                 + _bdot(attn[i], jnp.concatenate(v_new_c[i], axis=0)))
        hh = chains[i][0]
        o_refs[chains[i][1]][pl.ds(r0s[i], GROUP), hh * dh:(hh + 1) * dh] = o_all
    return states


def _dn_kernel(q_s, k_s, v_s, gb_ref, gr_ref, s0f_ref, s0b_ref, of_ref, ob_ref, sf_ref, sb_ref, *, seq_len):
    ng = seq_len // GROUP
    nh = DN_HEADS

    def body(m, carry):
        chains = [(hh, d, m if d == 0 else ng - 1 - m) for hh in range(nh) for d in (0, 1)]
        return tuple(_dn_groups(chains, list(carry), q_s, k_s, v_s, gb_ref, gr_ref, (of_ref, ob_ref)))

    init = tuple(r[hh] for hh in range(nh) for r in (s0f_ref, s0b_ref))
    fin = lax.fori_loop(0, ng, body, init)
    for hh in range(nh):
        sf_ref[hh] = fin[2 * hh]
        sb_ref[hh] = fin[2 * hh + 1]


def _deltanet(qkv, gb, grow, s0f, s0b, bsz, seq_len):
    h, dh = DN_HEADS, DN_HEAD_DIM
    ng = seq_len // GROUP
    w = h * dh
    once = pl.Buffered(1)
    col = lambda j: pl.BlockSpec((seq_len, w), lambda b: (b, j), pipeline_mode=once)
    st = pl.BlockSpec((None, h, dh, dh), lambda b: (b, 0, 0, 0))
    return pl.pallas_call(
        functools.partial(_dn_kernel, seq_len=seq_len),
        grid=(bsz,),
        in_specs=[col(0), col(1), col(2),
                  pl.BlockSpec((seq_len, 4 * h), lambda b: (b, 0), pipeline_mode=once),
                  pl.BlockSpec((None, h, 2, ng, GROUP), lambda b: (b, 0, 0, 0, 0)),
                  st, st],
        out_specs=[col(0), col(0), st, st],
        out_shape=[jax.ShapeDtypeStruct((bsz * seq_len, w), F32),
                   jax.ShapeDtypeStruct((bsz * seq_len, w), F32),
                   jax.ShapeDtypeStruct((bsz, h, dh, dh), F32),
                   jax.ShapeDtypeStruct((bsz, h, dh, dh), F32)],
        compiler_params=_cparams("parallel"),
        name="deltanet",
    )(qkv, qkv, qkv, gb, grow, s0f, s0b)


def _dft_table_kernel(tac_ref, tas_ref, tbc_ref, tbs_ref, cp_ref, sp_ref, *, nt1):
    tbc = tbc_ref[...]
    tbs = tbs_ref[...]
    for t1 in range(nt1):
        ac = tac_ref[:, t1:t1 + 1]
        asn = tas_ref[:, t1:t1 + 1]
        cp_ref[:, t1 * LANES:(t1 + 1) * LANES] = (ac * tbc - asn * tbs).astype(BF16)
        sp_ref[:, t1 * LANES:(t1 + 1) * LANES] = (asn * tbc + ac * tbs).astype(BF16)


def _dft_tables(n):
    per = n // LANES
    nt1 = per // 2
    s = np.arange(n, dtype=np.int64)[:, None]
    t1 = np.arange(nt1, dtype=np.int64)[None, :]
    t0 = np.arange(LANES, dtype=np.int64)[None, :]
    ang_a = 2.0 * np.pi * ((s * t1) % per) / per
    ang_b = 2.0 * np.pi * ((s * t0) % n) / n
    pad = ((0, 0), (0, LANES - nt1))
    tac = jnp.asarray(np.pad(np.cos(ang_a), pad), F32)
    tas = jnp.asarray(np.pad(np.sin(ang_a), pad), F32)
    tbc = jnp.asarray(np.cos(ang_b), F32)
    tbs = jnp.asarray(np.sin(ang_b), F32)
    tr = min(n, 256)
    small = pl.BlockSpec((tr, LANES), lambda i: (i, 0))
    big = pl.BlockSpec((tr, n // 2), lambda i: (i, 0))
    return pl.pallas_call(
        functools.partial(_dft_table_kernel, nt1=nt1),
        grid=(n // tr,),
        in_specs=[small, small, small, small],
        out_specs=[big, big],
        out_shape=[jax.ShapeDtypeStruct((n, n // 2), BF16)] * 2,
        compiler_params=_cparams("parallel"),
        name="dft_tables",
    )(tac, tas, tbc, tbs)


MIRROR_HEAD = 2 * SUBLANES


def _fourier_kernel(f_ref, p1_ref, p2_ref, mid_ref, cp_ref, sp_ref, cc_ref, sc_ref, o_ref):
    kstep = pl.program_id(1)
    tk = f_ref.shape[0]
    seq_len = o_ref.shape[0]
    cc = cc_ref[...]
    sc = sc_ref[...]

    def chan(xb, tab):
        return jnp.concatenate([jnp.dot(xb[:, g * LANES:(g + 1) * LANES], tab, preferred_element_type=F32)
                                for g in range(FOURIER_GROUPS)], axis=1)

    xb = f_ref[...].astype(BF16)
    pb = jnp.concatenate([p1_ref[...], p2_ref[...]], axis=0).astype(BF16)
    ri = lax.broadcasted_iota(jnp.int32, (tk, tk + MIRROR_HEAD), 0)
    ci = lax.broadcasted_iota(jnp.int32, (tk, tk + MIRROR_HEAD), 1)
    sel = ((ri >= 1) & (ci == tk - ri)) | ((ri == 0) & (ci == tk) & (kstep > 0))
    xr = jnp.dot(jnp.where(sel, 1.0, 0.0).astype(BF16), pb, preferred_element_type=F32)
    xf = xb.astype(F32)
    zc = chan((xf + xr).astype(BF16), cc).astype(BF16)
    zs = chan((xf - xr).astype(BF16), sc).astype(BF16)
    part = (jnp.dot(cp_ref[...], zc, preferred_element_type=F32)
            - jnp.dot(sp_ref[...], zs, preferred_element_type=F32))

    @pl.when(kstep == 0)
    def _():
        zmid = chan(mid_ref[...].astype(BF16), cc)[0:1, :]
        s_idx = lax.broadcasted_iota(jnp.int32, (seq_len, 1), 0)
        o_ref[...] = part + jnp.where(s_idx % 2 == 0, 1.0, -1.0) * zmid

    @pl.when(kstep > 0)
    def _():
        o_ref[...] += part


def _fourier(f, cp, sp, cc, sc, bsz, seq_len):
    t, fw = f.shape
    tk = min(seq_len // 2, 512)
    nk = seq_len // tk
    hpb = tk // MIRROR_HEAD
    last_head = t // MIRROR_HEAD - 1
    return pl.pallas_call(
        _fourier_kernel,
        grid=(bsz, nk // 2),
        in_specs=[pl.BlockSpec((tk, fw), lambda b, k: (b * nk + k, 0)),
                  pl.BlockSpec((tk, fw), lambda b, k: (b * nk + nk - 1 - k, 0)),
                  pl.BlockSpec((MIRROR_HEAD, fw), lambda b, k: (jnp.minimum((b * nk + nk - k) * hpb, last_head), 0)),
                  pl.BlockSpec((MIRROR_HEAD, fw), lambda b, k: ((b * nk + nk // 2) * hpb, 0)),
                  pl.BlockSpec((seq_len, tk), lambda b, k: (0, k)),
                  pl.BlockSpec((seq_len, tk), lambda b, k: (0, k)),
                  pl.BlockSpec((LANES, LANES), lambda b, k: (0, 0)),
                  pl.BlockSpec((LANES, LANES), lambda b, k: (0, 0))],
        out_specs=pl.BlockSpec((seq_len, fw), lambda b, k: (b, 0), pipeline_mode=pl.Buffered(1)),
        out_shape=jax.ShapeDtypeStruct((bsz * seq_len, fw), F32),
        compiler_params=_cparams("parallel", "arbitrary"),
        name="fourier",
    )(f, f, f, f, cp, sp, cc, sc)


def _outproj_kernel(four_ref, of_ref, ob_ref, z_ref, x_ref, gout_ref, gt1_ref, sc2_ref, sh2_ref, g2_ref,
                    wout_ref, wr_ref, br_ref, cnt0_ref, xn_ref, h2_ref, ti_ref, tg_ref, rk_ref, cnt_ref,
                    carry, *, fw):
    @pl.when(pl.program_id(0) == 0)
    def _():
        carry[...] = cnt0_ref[...]

    o = of_ref[...] + ob_ref[...]
    z = z_ref[...]
    gout = gout_ref[...]
    parts = []
    for hh in range(DN_HEADS):
        sl = slice(hh * DN_HEAD_DIM, (hh + 1) * DN_HEAD_DIM)
        oh = o[:, sl]
        zh = z[:, sl]
        oh = oh * lax.rsqrt(jnp.mean(oh * oh, axis=-1, keepdims=True) + EPS) * gout
        parts.append(oh * (zh * _sigmoid(zh)))
    gated = jnp.concatenate(parts, axis=1).astype(BF16)
    mix = (jnp.dot(four_ref[...].astype(BF16), wout_ref[:fw, :], preferred_element_type=F32)
           + jnp.dot(gated, wout_ref[fw:, :], preferred_element_type=F32))
    xn = x_ref[...] + gt1_ref[...] * mix
    xn_ref[...] = xn
    inv = lax.rsqrt(jnp.mean(xn * xn, axis=-1, keepdims=True) + EPS)
    h2 = (xn * inv * g2_ref[...]) * (1.0 + sc2_ref[...]) + sh2_ref[...]
    tm = h2.shape[0]
    for s in range(h2.shape[1] // LANES):
        h2_ref[pl.ds(s, tm, stride=SUBLANES), :] = h2[:, s * LANES:(s + 1) * LANES]
    wr = wr_ref[...]
    wr_hi = wr.astype(BF16)
    wr_lo = (wr - wr_hi.astype(F32)).astype(BF16)
    h2_hi = h2.astype(BF16)
    h2_lo = (h2 - h2_hi.astype(F32)).astype(BF16)
    logits = (jnp.dot(h2_hi, wr_hi, preferred_element_type=F32) + jnp.dot(h2_hi, wr_lo, preferred_element_type=F32)
              + jnp.dot(h2_lo, wr_hi, preferred_element_type=F32)) + br_ref[...]
    lane = lax.broadcasted_iota(jnp.int32, logits.shape, 1)
    neg = jnp.float32(-jnp.inf)
    logits = jnp.where(lane < N_EXPERTS, logits, neg)
    vals, idxs = [], []
    for _ in range(TOP_K):
        m = jnp.max(logits, axis=-1, keepdims=True)
        idx = jnp.min(jnp.where(logits == m, lane, LANES), axis=-1, keepdims=True)
        vals.append(m)
        idxs.append(idx)
        logits = jnp.where(lane == idx, neg, logits)
    ex = [jnp.exp(vv - vals[0]) for vv in vals]
    den = ex[0] + ex[1] + ex[2] + ex[3]
    onehot = jnp.zeros(logits.shape, F32)
    for j in range(TOP_K):
        onehot = onehot + jnp.where(lane == idxs[j], 1.0, 0.0)
    ri = lax.broadcasted_iota(jnp.int32, (tm, tm), 0)
    ci = lax.broadcasted_iota(jnp.int32, (tm, tm), 1)
    tri = jnp.where(ri > ci, 1.0, 0.0).astype(BF16)
    base = jnp.dot(tri, onehot.astype(BF16), preferred_element_type=F32) + carry[...]
    carry[...] = carry[...] + jnp.sum(onehot, axis=0, keepdims=True)
    cnt_ref[...] = carry[...]
    kl = lax.broadcasted_iota(jnp.int32, (tm, TOP_K), 1)
    ti = jnp.zeros((tm, TOP_K), jnp.int32)
    tg = jnp.zeros((tm, TOP_K), F32)
    rk = jnp.zeros((tm, TOP_K), jnp.int32)
    for j in range(TOP_K):
        rank_j = jnp.sum(jnp.where(lane == idxs[j], base, 0.0), axis=-1, keepdims=True).astype(jnp.int32)
        ti = jnp.where(kl == j, idxs[j], ti)
        tg = jnp.where(kl == j, ex[j] / den, tg)
        rk = jnp.where(kl == j, rank_j, rk)
    ti_ref[...] = ti
    tg_ref[...] = tg
    rk_ref[...] = rk


def _outproj(four, o_f, o_b, z, x, g_out, gt1, sc2, sh2, g2, w_out, w_r, b_r, cnt0, rows_per_group, tm=512):
    t, d = x.shape
    fw = four.shape[1]
    ng = gt1.shape[0]
    tpg = rows_per_group // tm
    grp = lambda i: (jnp.minimum(i // tpg, ng - 1), 0, 0)
    const = lambda i: (0, 0)
    row = lambda w: pl.BlockSpec((tm, w), lambda i: (i, 0))
    gspec = pl.BlockSpec((None, 1, d), grp)
    return pl.pallas_call(
        functools.partial(_outproj_kernel, fw=fw),
        grid=(t // tm,),
        in_specs=[row(fw), row(z.shape[1]), row(z.shape[1]), row(z.shape[1]), row(d),
                  pl.BlockSpec((1, DN_HEAD_DIM), const), gspec, gspec, gspec,
                  pl.BlockSpec((1, d), const),
                  pl.BlockSpec(w_out.shape, const),
                  pl.BlockSpec(w_r.shape, const),
                  pl.BlockSpec((1, LANES), const),
                  pl.BlockSpec((1, LANES), const)],
        out_specs=[row(d),
                   pl.BlockSpec((tm * SUBLANES, LANES), lambda i: (i, 0)),
                   row(TOP_K), row(TOP_K), row(TOP_K),
                   pl.BlockSpec((1, LANES), const)],
        out_shape=[jax.ShapeDtypeStruct((t, d), F32),
                   jax.ShapeDtypeStruct((t * d // LANES, LANES), F32),
                   jax.ShapeDtypeStruct((t, TOP_K), jnp.int32),
                   jax.ShapeDtypeStruct((t, TOP_K), F32),
                   jax.ShapeDtypeStruct((t, TOP_K), jnp.int32),
                   jax.ShapeDtypeStruct((1, LANES), F32)],
        scratch_shapes=[pltpu.VMEM((1, LANES), F32)],
        compiler_params=_cparams("arbitrary"),
        name="outproj",
    )(four, o_f, o_b, z, x, g_out.reshape(1, DN_HEAD_DIM), gt1, sc2, sh2, g2.reshape(1, d), w_out, w_r, b_r, cnt0)


TOK_UNROLL = 4


def _row_copy(src, src_row, dst, dst_row, sem):
    return pltpu.make_async_copy(
        src.at[pl.ds(pl.multiple_of(src_row * SUBLANES, SUBLANES), SUBLANES), :],
        dst.at[pl.ds(pl.multiple_of(dst_row * SUBLANES, SUBLANES), SUBLANES), :], sem)


def _dispatch_kernel(cnt_ref, ps_ref, pd_ref, nu_ref, *refs, tm, steps, n_blocks):
    ns = len(steps)
    dest_refs, h_refs = refs[0:2 * ns:2], refs[1:2 * ns:2]
    xs_hbm, zero_s, sem, sem_pad = refs[2 * ns:]
    i = pl.program_id(0)
    blk_rows = MOE_BLOCK * SUBLANES

    def tail_copy(b):
        return pltpu.make_async_copy(zero_s, xs_hbm.at[pl.ds(pl.multiple_of(b * blk_rows, blk_rows), blk_rows), :],
                                     sem_pad)

    @pl.when(i == 0)
    def _():
        zero_s[...] = jnp.zeros(zero_s.shape, F32)

        def per_expert(e, tot):
            def one(r, c):
                _row_copy(zero_s, 0, xs_hbm, ps_ref[e] + r, sem_pad).start()
                return c

            lax.fori_loop(cnt_ref[e], pd_ref[e], one, 0)
            return tot + pd_ref[e] - cnt_ref[e]

        n_pad = lax.fori_loop(0, N_EXPERTS, per_expert, 0)

        def start_tail(b, c):
            tail_copy(b).start()
            return c

        lax.fori_loop(nu_ref[0], n_blocks, start_tail, 0)

        def wait_row(r, c):
            _row_copy(zero_s, 0, xs_hbm, 0, sem_pad).wait()
            return c

        lax.fori_loop(0, n_pad, wait_row, 0)

        def wait_tail(b, c):
            tail_copy(b).wait()
            return c

        lax.fori_loop(nu_ref[0], n_blocks, wait_tail, 0)

    first = 0
    for dest_ref, h_ref, nst in zip(dest_refs, h_refs, steps):
        @pl.when((i >= first) & (i < first + nst))
        def _(dest_ref=dest_ref, h_ref=h_ref):
            def issue(blk, carry):
                for tt in range(TOK_UNROLL):
                    t = blk * TOK_UNROLL + tt
                    for j in range(TOP_K):
                        _row_copy(h_ref, t, xs_hbm, dest_ref[0, 0, t * TOP_K + j], sem).start(priority=j % 2)
                return carry

            lax.fori_loop(0, tm // TOK_UNROLL, issue, 0)
            for _ in range(TOP_K):
                pltpu.make_async_copy(h_ref, xs_hbm.at[pl.ds(0, tm * SUBLANES), :], sem).wait()

        first += nst


def _dispatch(streams, counts, pstart, padded, n_used, n_blocks, tm=512):
    seg = streams[0][0].shape[0] // streams[0][1].shape[0]
    steps = [dest.shape[0] // tm for _, dest in streams]
    in_specs, args, first = [], [], 0
    for (h_rt, dest), nst in zip(streams, steps):
        idx = lambda i, *_, first=first, nst=nst: (jnp.clip(i - first, 0, nst - 1), 0)
        idx3 = lambda i, *_, idx=idx: idx(i) + (0,)
        in_specs += [pl.BlockSpec((1, 1, tm * TOP_K), idx3, memory_space=pltpu.SMEM),
                     pl.BlockSpec((tm * seg, LANES), idx)]
        args += [dest.reshape(nst, 1, tm * TOP_K), h_rt]
        first += nst
    grid_spec = pltpu.PrefetchScalarGridSpec(
        num_scalar_prefetch=4,
        grid=(first,),
        in_specs=in_specs,
        out_specs=pl.BlockSpec(memory_space=pltpu.HBM),
        scratch_shapes=[pltpu.VMEM((MOE_BLOCK * seg, LANES), F32),
                        pltpu.SemaphoreType.DMA(()), pltpu.SemaphoreType.DMA(())],
    )
    return pl.pallas_call(
        functools.partial(_dispatch_kernel, tm=tm, steps=tuple(steps), n_blocks=n_blocks),
        grid_spec=grid_spec,
        out_shape=jax.ShapeDtypeStruct((n_blocks * MOE_BLOCK * seg, LANES), F32),
        compiler_params=_cparams("arbitrary"),
        name="dispatch",
    )(counts, pstart, padded, n_used, *args)


def _expert_kernel(be_ref, nu_ref, xs_ref, wg_ref, bg_ref, wu_ref, bu_ref, wd_ref, bd_ref, y_ref,
                   x2d, wg_b, wu_b, wd_b):
    i = pl.program_id(0)
    rows = MOE_BLOCK
    nseg = x2d.shape[1] // LANES

    @pl.when(i < nu_ref[0])
    def _():
        @pl.when((i == 0) | (be_ref[i] != be_ref[jnp.maximum(i - 1, 0)]))
        def _():
            wg_b[...] = wg_ref[...].astype(BF16)
            wu_b[...] = wu_ref[...].astype(BF16)
            wd_b[...] = wd_ref[...].astype(BF16)

        for s in range(nseg):
            x2d[:, s * LANES:(s + 1) * LANES] = xs_ref[pl.ds(s, rows, stride=SUBLANES), :].astype(BF16)
        x = x2d[...]
        a = jnp.minimum(jnp.dot(x, wg_b[...], preferred_element_type=F32) + bg_ref[...], SWIGLU_LIMIT)
        u = jnp.clip(jnp.dot(x, wu_b[...], preferred_element_type=F32) + bu_ref[...], -SWIGLU_LIMIT, SWIGLU_LIMIT)
        act = (a * _sigmoid(SWIGLU_ALPHA * a) * (u + 1.0)).astype(BF16)
        y = jnp.dot(act, wd_b[...], preferred_element_type=F32) + bd_ref[...]
        for s in range(nseg):
            y_ref[pl.ds(s, rows, stride=SUBLANES), :] = y[:, s * LANES:(s + 1) * LANES]

    @pl.when(i >= nu_ref[0])
    def _():
        y_ref[...] = jnp.zeros(y_ref.shape, F32)


def _experts(xs, block_e, n_used, layer, w_gate, b_gate, w_up, b_up, w_down, b_down):
    n_blocks = block_e.shape[0]
    depth, ne, d, ff = w_gate.shape
    seg = d // LANES
    blk = lambda i, nu: jnp.minimum(i, jnp.maximum(nu[0] - 1, 0))
    wspec = lambda shp: pl.BlockSpec((None, None) + shp, lambda i, be, nu: (layer, be[blk(i, nu)], 0, 0))
    grid_spec = pltpu.PrefetchScalarGridSpec(
        num_scalar_prefetch=2,
        grid=(n_blocks,),
        in_specs=[pl.BlockSpec((MOE_BLOCK * seg, LANES), lambda i, be, nu: (blk(i, nu), 0)),
                  wspec((d, ff)), wspec((1, ff)), wspec((d, ff)), wspec((1, ff)), wspec((ff, d)), wspec((1, d))],
        out_specs=pl.BlockSpec((MOE_BLOCK * seg, LANES), lambda i, be, nu: (i, 0)),
        scratch_shapes=[pltpu.VMEM((MOE_BLOCK, d), BF16), pltpu.VMEM((d, ff), BF16), pltpu.VMEM((d, ff), BF16),
                        pltpu.VMEM((ff, d), BF16)],
    )
    return pl.pallas_call(
        _expert_kernel,
        grid_spec=grid_spec,
        out_shape=jax.ShapeDtypeStruct((n_blocks * MOE_BLOCK * seg, LANES), F32),
        compiler_params=_cparams("arbitrary"),
        name="experts",
    )(block_e, n_used, xs, w_gate, b_gate.reshape(depth, ne, 1, ff), w_up, b_up.reshape(depth, ne, 1, ff),
      w_down, b_down.reshape(depth, ne, 1, d))


def _combine_kernel(dest_ref, dnext_ref, y_hbm, x_ref, tg_ref, gt_ref, gf_ref, o_ref, buf0, buf1, sem0, sem1, *,
                    tm, final_norm):
    n = tm * TOP_K
    i = pl.program_id(0)
    last = pl.num_programs(0) - 1

    def gather(d_ref, buf, sem):
        def issue(blk, carry):
            for tt in range(TOK_UNROLL):
                t = blk * TOK_UNROLL + tt
                for j in range(TOP_K):
                    _row_copy(y_hbm, d_ref[0, 0, t * TOP_K + j], buf, j * tm + t, sem).start(priority=j % 2)
            return carry

        lax.fori_loop(0, tm // TOK_UNROLL, issue, 0)

    def reduce(buf, sem):
        pltpu.make_async_copy(y_hbm.at[pl.ds(0, n * SUBLANES), :], buf, sem).wait()
        gt = gt_ref[...]
        tg = tg_ref[...]
        d = o_ref.shape[1]
        ssq = jnp.zeros((tm, 1), F32)
        for s in range(d // LANES):
            acc = tg[:, 0:1] * buf[pl.ds(s, tm, stride=SUBLANES), :]
            for j in range(1, TOP_K):
                acc = acc + tg[:, j:j + 1] * buf[pl.ds(j * tm * SUBLANES + s, tm, stride=SUBLANES), :]
            sl = slice(s * LANES, (s + 1) * LANES)
            o = x_ref[:, sl] + gt[:, sl] * acc
            o_ref[:, sl] = o
            if final_norm:
                ssq = ssq + jnp.sum(o * o, axis=-1, keepdims=True)
        if final_norm:
            inv = lax.rsqrt(ssq / d + EPS)
            o_ref[...] = o_ref[...] * inv * gf_ref[...]

    @pl.when(i == 0)
    def _():
        gather(dest_ref, buf0, sem0)

    for par, (cur, nxt) in enumerate((((buf0, sem0), (buf1, sem1)), ((buf1, sem1), (buf0, sem0)))):
        @pl.when(i % 2 == par)
        def _(cur=cur, nxt=nxt):
            @pl.when(i < last)
            def _():
                gather(dnext_ref, *nxt)

            reduce(*cur)


def _combine(x, y_rt, dest, top_g, gt2, g_final, final_norm, rows_per_group, tm=256):
    t, d = x.shape
    ng = gt2.shape[0]
    nt = t // tm
    tpg = rows_per_group // tm
    dest3 = dest.reshape(nt, 1, tm * TOP_K)
    buf = pltpu.VMEM((tm * TOP_K * SUBLANES, LANES), F32)
    return pl.pallas_call(
        functools.partial(_combine_kernel, tm=tm, final_norm=final_norm),
        grid=(nt,),
        in_specs=[pl.BlockSpec((1, 1, tm * TOP_K), lambda i: (i, 0, 0), memory_space=pltpu.SMEM),
                  pl.BlockSpec((1, 1, tm * TOP_K), lambda i: (jnp.minimum(i + 1, nt - 1), 0, 0),
                               memory_space=pltpu.SMEM),
                  pl.BlockSpec(memory_space=pltpu.HBM),
                  pl.BlockSpec((tm, d), lambda i: (i, 0)),
                  pl.BlockSpec((tm, TOP_K), lambda i: (i, 0)),
                  pl.BlockSpec((None, 1, d), lambda i: (jnp.minimum(i // tpg, ng - 1), 0, 0)),
                  pl.BlockSpec((1, d), lambda i: (0, 0))],
        out_specs=pl.BlockSpec((tm, d), lambda i: (i, 0)),
        out_shape=jax.ShapeDtypeStruct((t, d), F32),
        scratch_shapes=[buf, buf, pltpu.SemaphoreType.DMA(()), pltpu.SemaphoreType.DMA(())],
        compiler_params=_cparams("arbitrary"),
        name="combine",
    )(dest3, dest3, y_rt, x, top_g, gt2, g_final.reshape(1, d))


def _slot_tables(cnt_row, n_tok):
    n_blocks = -(-n_tok * TOP_K // MOE_BLOCK) + N_EXPERTS
    counts = cnt_row[0, :N_EXPERTS].astype(jnp.int32)
    padded = (counts + MOE_BLOCK - 1) // MOE_BLOCK * MOE_BLOCK
    padded_end = jnp.cumsum(padded)
    pstart = padded_end - padded
    blk_start = jnp.arange(n_blocks, dtype=jnp.int32) * MOE_BLOCK
    block_e = jnp.minimum(jnp.sum(padded_end[None, :] <= blk_start[:, None], axis=1), N_EXPERTS - 1).astype(jnp.int32)
    n_used = (padded_end[-1:] // MOE_BLOCK).astype(jnp.int32)
    return counts, pstart.astype(jnp.int32), padded.astype(jnp.int32), block_e, n_used, n_blocks


def _slot_of(top_i, rank, pstart):
    e = jnp.arange(N_EXPERTS, dtype=jnp.int32)
    return rank + jnp.sum(jnp.where(top_i[..., None] == e, pstart, 0), axis=-1)


def _decay_rows(gb, bsz, seq_len):
    g5 = gb.reshape(bsz, seq_len, 2, 2, DN_HEADS)
    return g5[:, :, :, 0, :].transpose(0, 3, 2, 1).reshape(bsz, DN_HEADS, 2, seq_len // GROUP, GROUP)


def kernel(x, c, ctx, c_ctx, w_mod, b_mod, g_norm1, w_in, conv_w, a_log, dt_bias, g_out_norm, w_out, g_norm2,
           w_router, b_router, w_gate, b_gate, w_up, b_up, w_down, b_down, g_final):
    bsz, seq, d = x.shape
    clen = ctx.shape[1]
    depth = w_mod.shape[0]
    fw = FOURIER_GROUPS * LANES
    nmain = fw + 4 * DN_HEADS * DN_HEAD_DIM
    nab = 4 * DN_HEADS

    xt = x.reshape(bsz * seq, d)
    ct = ctx.reshape(bsz * clen, d)
    c_rows = jnp.concatenate([c, c_ctx[None, :], jnp.zeros((2 * SUBLANES - bsz - 1, d), F32)], axis=0)

    cp_x, sp_x = _dft_tables(seq)
    cp_c, sp_c = _dft_tables(clen)
    ch = np.arange(LANES, dtype=np.int64)
    ang = 2.0 * np.pi * ((ch[:, None] * ch[None, :]) % LANES) / LANES

    def chan(tab, n):
        return jnp.asarray(tab / math.sqrt(n * LANES), F32).astype(BF16)

    lanepad = lambda v: jnp.pad(v, ((0, 0), (0, LANES - v.shape[1])))
    zeros4 = jnp.zeros((DN_HEADS,), F32)
    isdec_row = lanepad(jnp.tile(jnp.concatenate([jnp.ones((DN_HEADS,), F32), zeros4]), 2)[None, :])

    for l in range(depth):
        update_ctx = l < depth - 1
        mod = _mod(c_rows, w_mod[l], b_mod[l]).reshape(2 * SUBLANES, 6, 1, d)
        mx = lambda j: mod[:bsz, j]
        mc = lambda j: mod[bsz:bsz + 1, j]

        w_main = w_in[l][:, :nmain].astype(BF16)
        w_ab = lanepad(w_in[l][:, nmain:]).astype(BF16)
        alog_row = lanepad(jnp.concatenate([a_log[l][0], zeros4, a_log[l][1], zeros4])[None, :])
        dtb_row = lanepad(jnp.concatenate([dt_bias[l][0], zeros4, dt_bias[l][1], zeros4])[None, :])
        ip = lambda tok, sh, sc, rpg, slen: _inproj(tok, g_norm1[l], sc, sh, w_main, w_ab, conv_w[l], alog_row,
                                                     dtb_row, isdec_row, rpg, slen)
        f_x, qkv_x, z_x, gb_x = ip(xt, mx(0), mx(1), seq, seq)
        f_c, qkv_c, z_c, gb_c = ip(ct, mc(0), mc(1), bsz * clen, clen)

        s_zero = jnp.zeros((bsz, DN_HEADS, DN_HEAD_DIM, DN_HEAD_DIM), F32)
        ocf, ocb, s_f, s_b = _deltanet(qkv_c, gb_c, _decay_rows(gb_c, bsz, clen), s_zero, s_zero, bsz, clen)
        oxf, oxb, _, _ = _deltanet(qkv_x, gb_x, _decay_rows(gb_x, bsz, seq), s_f, s_b, bsz, seq)

        w_out_b = w_out[l].astype(BF16)
        w_r = lanepad(w_router[l])
        b_r = lanepad(b_router[l][None, :])
        four_x = _fourier(f_x, cp_x, sp_x, chan(np.cos(ang), seq), chan(np.sin(ang), seq), bsz, seq)
        cnt0 = jnp.zeros((1, LANES), F32)
        xn_x, h2_x, ti_x, tg_x, rk_x, cnt = _outproj(four_x, oxf, oxb, z_x, xt, g_out_norm[l], mx(2), mx(4), mx(3),
                                                      g_norm2[l], w_out_b, w_r, b_r, cnt0, seq)
        n_tok = bsz * seq
        if update_ctx:
            four_c = _fourier(f_c, cp_c, sp_c, chan(np.cos(ang), clen), chan(np.sin(ang), clen), bsz, clen)
            xn_c, h2_c, ti_c, tg_c, rk_c, cnt = _outproj(four_c, ocf, ocb, z_c, ct, g_out_norm[l], mc(2), mc(4),
                                                          mc(3), g_norm2[l], w_out_b, w_r, b_r, cnt, bsz * clen)
            n_tok += bsz * clen

        counts, pstart, padded, block_e, n_used, n_blocks = _slot_tables(cnt, n_tok)
        dest_x = _slot_of(ti_x, rk_x, pstart)
        streams = [(h2_x, dest_x)]
        if update_ctx:
            dest_c = _slot_of(ti_c, rk_c, pstart)
            streams.append((h2_c, dest_c))
        xs = _dispatch(streams, counts, pstart, padded, n_used, n_blocks)
        y_rt = _experts(xs, block_e, n_used, l, w_gate, b_gate, w_up, b_up, w_down, b_down)
        xt = _combine(xn_x, y_rt, dest_x, tg_x, mx(5), g_final, l == depth - 1, seq)
        if update_ctx:
            ct = _combine(xn_c, y_rt, dest_c, tg_c, mc(5), g_final, False, bsz * clen)

    return xt.reshape(bsz, seq, d)
```
